```python
import math
import jax, jax.numpy as jnp
from jax import lax
import numpy as np

D_MODEL = 2048
BATCH = 4
SEQ = 2048
DEPTH = 4
DEC_BATCH = 32
DEC_SEQ = 4
PAST_LEN = 16384
PAGE_SIZE = 128

N_META = 16
EPS = 1e-6
NEG_INF = -1e30
A_HEADS = 16
A_KV_HEADS = 8
A_GROUP = A_HEADS // A_KV_HEADS
A_HEAD_DIM = D_MODEL // A_HEADS // 2
A_Q_DIM = A_HEADS * 2 * A_HEAD_DIM
A_KV_DIM = A_KV_HEADS * 2 * A_HEAD_DIM
Q_BLOCK = 128
B_HEADS = 32
B_KV_HEADS = 8
B_GROUP = B_HEADS // B_KV_HEADS
B_HEAD_DIM = D_MODEL // B_HEADS
B_Q_DIM = B_HEADS * B_HEAD_DIM
B_KV_DIM = B_KV_HEADS * B_HEAD_DIM
WINDOW = 128
D_FF = 5632
CONV_WIDTH = 3
N_A_LAYERS = (DEPTH + 1) // 2
N_B_LAYERS = DEPTH // 2

kernel_name = 'diff_swa_sink_convffn_hybrid_step'


def rms_norm(x, g):
    xf = x.astype(jnp.float32)
    y = xf * lax.rsqrt(jnp.mean(xf * xf, axis=-1, keepdims=True) + EPS)
    return (y * g.astype(jnp.float32)).astype(x.dtype)


def alibi_slopes(n):
    return jnp.power(2.0, -8.0 * jnp.arange(1, n + 1, dtype=jnp.float32) / n)


def pad_seq(a, front, back):
    widths = [(0, 0)] * a.ndim
    widths[1] = (front, back)
    return jnp.pad(a, widths)


def diff_qkv(xn, w_qkv):
    B, T, _ = xn.shape
    qkv = xn @ w_qkv
    q, k, v = jnp.split(qkv, [A_Q_DIM, A_Q_DIM + A_KV_DIM], axis=-1)
    q = q.reshape(B, T, A_KV_HEADS, A_GROUP, 2, A_HEAD_DIM)
    k = k.reshape(B, T, A_KV_HEADS, 2 * A_HEAD_DIM)
    v = v.reshape(B, T, A_KV_HEADS, 2 * A_HEAD_DIM)
    return q, k, v


def diff_lambda(lp, lam_init):
    lpf = lp.astype(jnp.float32)
    return jnp.exp(jnp.sum(lpf[0] * lpf[1])) - jnp.exp(jnp.sum(lpf[2] * lpf[3])) + lam_init


def diff_attend(q, k, v, q_pos, k_pos, k_valid, lam, slopes):
    B, S = k.shape[:2]
    k = k.reshape(B, S, A_KV_HEADS, 2, A_HEAD_DIM)
    s = jnp.einsum('btkgcd,bskcd->bkgcts', q, k,
                   preferred_element_type=jnp.float32) * (A_HEAD_DIM ** -0.5)
    dist = (q_pos[:, None] - k_pos[None, :]).astype(jnp.float32)
    sl = slopes.reshape(A_KV_HEADS, A_GROUP)[None, :, :, None, None, None]
    s = s - sl * dist
    mask = (dist >= 0) & k_valid[None, :]
    s = jnp.where(mask, s, NEG_INF)
    p = jax.nn.softmax(s, axis=-1)
    w = p[:, :, :, 0] - lam * p[:, :, :, 1]
    return jnp.einsum('bkgts,bskd->btkgd', w.astype(v.dtype), v)


def diff_prompt(q, k, v, lam, slopes):
    B, L = q.shape[:2]
    n_blk = -(-L // Q_BLOCK)
    Lp = n_blk * Q_BLOCK
    qp = pad_seq(q, 0, Lp - L)
    kp = pad_seq(k, 0, Lp - L)
    vp = pad_seq(v, 0, Lp - L)
    k_pos = jnp.arange(Lp)
    k_valid = k_pos < L
    qb = jnp.swapaxes(qp.reshape(B, n_blk, Q_BLOCK, A_KV_HEADS, A_GROUP, 2, A_HEAD_DIM), 0, 1)

    def block(args):
        qi, j = args
        q_pos = j * Q_BLOCK + jnp.arange(Q_BLOCK)
        return diff_attend(qi, kp, vp, q_pos, k_pos, k_valid, lam, slopes)

    o = lax.map(block, (qb, jnp.arange(n_blk)))
    o = jnp.swapaxes(o, 0, 1).reshape(B, Lp, A_KV_HEADS, A_GROUP, 2 * A_HEAD_DIM)
    return o[:, :L]


def diff_sample(q, k_new, v_new, k_pool, v_pool, page_table, lam, slopes):
    Bd, T = q.shape[:2]
    past = page_table.shape[1] * PAGE_SIZE
    k_past = k_pool[page_table].reshape(Bd, past, A_KV_HEADS, 2 * A_HEAD_DIM)
    v_past = v_pool[page_table].reshape(Bd, past, A_KV_HEADS, 2 * A_HEAD_DIM)
    k = jnp.concatenate([k_past.astype(k_new.dtype), k_new], axis=1)
    v = jnp.concatenate([v_past.astype(v_new.dtype), v_new], axis=1)
    q_pos = past + jnp.arange(T)
    k_pos = jnp.arange(past + T)
    k_valid = jnp.ones((past + T,), dtype=bool)
    return diff_attend(q, k, v, q_pos, k_pos, k_valid, lam, slopes)


def diff_out(o, subln_g, w_o, lam_init):
    B, T = o.shape[:2]
    o = rms_norm(o, subln_g) * (1.0 - lam_init)
    return o.reshape(B, T, A_Q_DIM) @ w_o


def swa_qkv(xn, w_qkv, b_qkv):
    B, T, _ = xn.shape
    qkv = xn @ w_qkv + b_qkv
    q, k, v = jnp.split(qkv, [B_Q_DIM, B_Q_DIM + B_KV_DIM], axis=-1)
    q = q.reshape(B, T, B_KV_HEADS, B_GROUP, B_HEAD_DIM)
    k = k.reshape(B, T, B_KV_HEADS, B_HEAD_DIM)
    v = v.reshape(B, T, B_KV_HEADS, B_HEAD_DIM)
    return q, k, v


def swa_attend(q, k, v, q_pos, k_pos, k_valid, sinks, slopes):
    s = jnp.einsum('bntkgd,bnskd->bnkgts', q, k,
                   preferred_element_type=jnp.float32) * (B_HEAD_DIM ** -0.5)
    dist = (q_pos[:, :, None] - k_pos[:, None, :]).astype(jnp.float32)
    sl = slopes.reshape(B_KV_HEADS, B_GROUP)[None, None, :, :, None, None]
    s = s - sl * dist[None, :, None, None]
    mask = (dist >= 0) & (dist <= WINDOW) & k_valid[:, None, :]
    s = jnp.where(mask[None, :, None, None], s, NEG_INF)
    sink = sinks.astype(jnp.float32).reshape(B_KV_HEADS, B_GROUP)[None, None, :, :, None, None]
    m = jnp.maximum(jnp.max(s, axis=-1, keepdims=True), sink)
    p = jnp.exp(s - m)
    p = p / (jnp.sum(p, axis=-1, keepdims=True) + jnp.exp(sink - m))
    return jnp.einsum('bnkgts,bnskd->bntkgd', p.astype(v.dtype), v)


def swa_prompt(q, k, v, sinks, slopes):
    B, L = q.shape[:2]
    front = (-N_META) % WINDOW
    back = (-(front + L)) % WINDOW
    Lp = front + L + back
    nb = Lp // WINDOW
    qb = pad_seq(q, front, back).reshape(B, nb, WINDOW, B_KV_HEADS, B_GROUP, B_HEAD_DIM)
    kb = pad_seq(k, front, back).reshape(B, nb, WINDOW, B_KV_HEADS, B_HEAD_DIM)
    vb = pad_seq(v, front, back).reshape(B, nb, WINDOW, B_KV_HEADS, B_HEAD_DIM)

    def band(a):
        prev = jnp.pad(a[:, :-1], ((0, 0), (1, 0)) + ((0, 0),) * (a.ndim - 2))
        return jnp.concatenate([prev, a], axis=2)

    pos = (jnp.arange(Lp) - front).reshape(nb, WINDOW)
    k_pos = jnp.concatenate([pos - WINDOW, pos], axis=1)
    k_valid = (k_pos >= 0) & (k_pos < L)
    o = swa_attend(qb, band(kb), band(vb), pos, k_pos, k_valid, sinks, slopes)
    return o.reshape(B, Lp, B_KV_HEADS, B_GROUP, B_HEAD_DIM)[:, front:front + L]


def swa_sample(q, k_new, v_new, k_buf, v_buf, past, sinks, slopes):
    T = q.shape[1]
    k = jnp.concatenate([k_buf.astype(k_new.dtype), k_new], axis=1)
    v = jnp.concatenate([v_buf.astype(v_new.dtype), v_new], axis=1)
    q_pos = (past + jnp.arange(T))[None]
    k_pos = (past - WINDOW + jnp.arange(WINDOW + T))[None]
    k_valid = jnp.ones((1, WINDOW + T), dtype=bool)
    o = swa_attend(q[:, None], k[:, None], v[:, None], q_pos, k_pos, k_valid, sinks, slopes)
    return o[:, 0], k[:, -WINDOW:], v[:, -WINDOW:]


def swa_out(o, w_o, b_o):
    B, T = o.shape[:2]
    return o.reshape(B, T, B_Q_DIM) @ w_o + b_o


def conv_ffn(xn, w_gu, conv_w, conv_b, w_down, prev):
    T = xn.shape[1]
    g, u = jnp.split(xn @ w_gu, 2, axis=-1)
    gp = jnp.concatenate([prev.astype(g.dtype), g], axis=1)
    gc = conv_b + sum(conv_w[j] * gp[:, j:j + T] for j in range(CONV_WIDTH))
    h = jax.nn.silu(gc) * u
    return h @ w_down, gp[:, -(CONV_WIDTH - 1):]


def setup_inputs(seed: int = 0) -> dict:
    key = jax.random.key(seed)
    ks = jax.random.split(key, 26)
    f32 = jnp.float32

    def nrm(k, shape, scale=1.0):
        return jax.random.normal(k, shape, f32) * scale

    n_pages = PAST_LEN // PAGE_SIZE
    n_used = DEC_BATCH * n_pages
    n_pool = n_used + n_used // 4
    page_table = jax.random.permutation(ks[7], n_pool)[:n_used].reshape(DEC_BATCH, n_pages).astype(jnp.int32)
    a_qkv = A_Q_DIM + 2 * A_KV_DIM
    b_qkv = B_Q_DIM + 2 * B_KV_DIM
    return {
        'x_prompt': nrm(ks[0], (BATCH, SEQ, D_MODEL)),
        'x_sample': nrm(ks[1], (DEC_BATCH, DEC_SEQ, D_MODEL)),
        'cache_a_k': nrm(ks[2], (N_A_LAYERS, n_pool, PAGE_SIZE, A_KV_HEADS, 2 * A_HEAD_DIM)),
        'cache_a_v': nrm(ks[3], (N_A_LAYERS, n_pool, PAGE_SIZE, A_KV_HEADS, 2 * A_HEAD_DIM)),
        'state_b_k': nrm(ks[4], (N_B_LAYERS, DEC_BATCH, WINDOW, B_KV_HEADS, B_HEAD_DIM)),
        'state_b_v': nrm(ks[5], (N_B_LAYERS, DEC_BATCH, WINDOW, B_KV_HEADS, B_HEAD_DIM)),
        'state_conv': nrm(ks[6], (DEPTH, DEC_BATCH, CONV_WIDTH - 1, D_FF)),
        'page_table': page_table,
        'meta': nrm(ks[8], (N_META, D_MODEL)),
        'norm_mix': 1.0 + nrm(ks[9], (DEPTH, D_MODEL), 0.02),
        'norm_ffn': 1.0 + nrm(ks[10], (DEPTH, D_MODEL), 0.02),
        'norm_final': 1.0 + nrm(ks[11], (D_MODEL,), 0.02),
        'w_qkv_a': nrm(ks[12], (N_A_LAYERS, D_MODEL, a_qkv), D_MODEL ** -0.5),
        'lambda_a': nrm(ks[13], (N_A_LAYERS, 4, A_HEAD_DIM), 0.1),
        'subln_a': 1.0 + nrm(ks[14], (N_A_LAYERS, 2 * A_HEAD_DIM), 0.02),
        'w_o_a': nrm(ks[15], (N_A_LAYERS, A_Q_DIM, D_MODEL), A_Q_DIM ** -0.5),
        'w_qkv_b': nrm(ks[16], (N_B_LAYERS, D_MODEL, b_qkv), D_MODEL ** -0.5),
        'b_qkv_b': nrm(ks[17], (N_B_LAYERS, b_qkv), 0.01),
        'sinks_b': nrm(ks[18], (N_B_LAYERS, B_HEADS), 0.5),
        'w_o_b': nrm(ks[19], (N_B_LAYERS, B_Q_DIM, D_MODEL), B_Q_DIM ** -0.5),
        'b_o_b': nrm(ks[20], (N_B_LAYERS, D_MODEL), 0.01),
        'w_gu': nrm(ks[21], (DEPTH, D_MODEL, 2 * D_FF), D_MODEL ** -0.5),
        'conv_w': nrm(ks[22], (DEPTH, CONV_WIDTH, D_FF), CONV_WIDTH ** -0.5),
        'conv_b': nrm(ks[23], (DEPTH, D_FF), 0.01),
        'w_down': nrm(ks[24], (DEPTH, D_FF, D_MODEL), D_FF ** -0.5),
    }


def reference(x_prompt, x_sample, cache_a_k, cache_a_v, state_b_k, state_b_v, state_conv,
              page_table, meta, norm_mix, norm_ffn, norm_final, w_qkv_a, lambda_a, subln_a,
              w_o_a, w_qkv_b, b_qkv_b, sinks_b, w_o_b, b_o_b, w_gu, conv_w, conv_b, w_down):
    B = x_prompt.shape[0]
    Bd = x_sample.shape[0]
    past = page_table.shape[1] * PAGE_SIZE
    slopes_a = alibi_slopes(A_HEADS)
    slopes_b = alibi_slopes(B_HEADS)
    hp = jnp.concatenate([jnp.broadcast_to(meta.astype(x_prompt.dtype)[None], (B, N_META, D_MODEL)),
                          x_prompt], axis=1)
    hs = x_sample
    ka_p, va_p, ka_s, va_s = [], [], [], []
    kb_p, vb_p, kb_s, vb_s = [], [], [], []
    cv_p, cv_s = [], []
    for i in range(DEPTH):
        xnp = rms_norm(hp, norm_mix[i])
        xns = rms_norm(hs, norm_mix[i])
        j = i // 2
        if i % 2 == 0:
            lam_init = 0.8 - 0.6 * math.exp(-0.3 * i)
            lam = diff_lambda(lambda_a[j], lam_init)
            q, k, v = diff_qkv(xnp, w_qkv_a[j])
            o = diff_prompt(q, k, v, lam, slopes_a)
            hp = hp + diff_out(o, subln_a[j], w_o_a[j], lam_init)
            ka_p.append(k)
            va_p.append(v)
            q, k, v = diff_qkv(xns, w_qkv_a[j])
            o = diff_sample(q, k, v, cache_a_k[j], cache_a_v[j], page_table, lam, slopes_a)
            hs = hs + diff_out(o, subln_a[j], w_o_a[j], lam_init)
            ka_s.append(k)
            va_s.append(v)
        else:
            q, k, v = swa_qkv(xnp, w_qkv_b[j], b_qkv_b[j])
            o = swa_prompt(q, k, v, sinks_b[j], slopes_b)
            hp = hp + swa_out(o, w_o_b[j], b_o_b[j])
            kb_p.append(k[:, -WINDOW:])
            vb_p.append(v[:, -WINDOW:])
            q, k, v = swa_qkv(xns, w_qkv_b[j], b_qkv_b[j])
            o, kn, vn = swa_sample(q, k, v, state_b_k[j], state_b_v[j], past, sinks_b[j], slopes_b)
            hs = hs + swa_out(o, w_o_b[j], b_o_b[j])
            kb_s.append(kn)
            vb_s.append(vn)
        zeros_prev = jnp.zeros((B, CONV_WIDTH - 1, D_FF), hp.dtype)
        f, c = conv_ffn(rms_norm(hp, norm_ffn[i]), w_gu[i], conv_w[i], conv_b[i], w_down[i], zeros_prev)
        hp = hp + f
        cv_p.append(c)
        f, c = conv_ffn(rms_norm(hs, norm_ffn[i]), w_gu[i], conv_w[i], conv_b[i], w_down[i], state_conv[i])
        hs = hs + f
        cv_s.append(c)
    y_prompt = rms_norm(hp, norm_final)[:, N_META:]
    y_sample = rms_norm(hs, norm_final)
    k_a_prompt = jnp.stack(ka_p)
    v_a_prompt = jnp.stack(va_p)
    k_a_sample = jnp.stack(ka_s)
    v_a_sample = jnp.stack(va_s)
    k_b_prompt = jnp.stack(kb_p)
    v_b_prompt = jnp.stack(vb_p)
    k_b_sample = jnp.stack(kb_s)
    v_b_sample = jnp.stack(vb_s)
    conv_prompt = jnp.stack(cv_p)
    conv_sample = jnp.stack(cv_s)
    return (y_prompt, y_sample, k_a_prompt, v_a_prompt, k_a_sample, v_a_sample,
            k_b_prompt, v_b_prompt, k_b_sample, v_b_sample, conv_prompt, conv_sample)
```

```python
import functools
import math

import numpy as np
import jax
import jax.numpy as jnp
from jax import lax
from jax.experimental import pallas as pl
from jax.experimental.pallas import tpu as pltpu

F32 = jnp.float32
BF16 = jnp.bfloat16

D_MODEL = 2048
BATCH = 4
SEQ = 2048
DEPTH = 4
DEC_BATCH = 32
DEC_SEQ = 4
PAGE = 128
N_META = 16
EPS = 1e-6
NEG_INF = -1e30
A_HEADS = 16
A_KV = 8
A_DH = 64
B_HEADS = 32
B_KV = 8
B_GROUP = 4
B_DH = 64
WINDOW = 128
D_FF = 5632

L_REAL = N_META + SEQ
BLK = 128
LP = -(-L_REAL // BLK) * BLK
N_BLK = LP // BLK
M_PROMPT = BATCH * LP
M_SAMPLE = DEC_BATCH * DEC_SEQ

VMEM_LIMIT = 56 * 1024 * 1024


def _params(sem, vmem=VMEM_LIMIT):
    return pltpu.CompilerParams(dimension_semantics=sem, vmem_limit_bytes=vmem)


def _rmsnorm_kernel(x_ref, g_ref, o_ref):
    x = x_ref[...]
    ms = jnp.mean(x * x, axis=-1, keepdims=True)
    o_ref[...] = (x * lax.rsqrt(ms + EPS) * g_ref[...]).astype(o_ref.dtype)


def rmsnorm(x, g, tm, out_dtype):
    m, d = x.shape
    return pl.pallas_call(
        _rmsnorm_kernel,
        grid=(m // tm,),
        in_specs=[pl.BlockSpec((tm, d), lambda i: (i, 0)),
                  pl.BlockSpec((1, d), lambda i: (0, 0))],
        out_specs=pl.BlockSpec((tm, d), lambda i: (i, 0)),
        out_shape=jax.ShapeDtypeStruct((m, d), out_dtype),
        compiler_params=_params(("parallel",)),
        name="rmsnorm",
    )(x, g.reshape(1, d))


def _matmul_kernel(*refs, has_bias, has_res):
    x_ref, w_ref = refs[0], refs[1]
    pos = 2
    b_ref = r_ref = None
    if has_bias:
        b_ref = refs[pos]
        pos += 1
    if has_res:
        r_ref = refs[pos]
        pos += 1
    o_ref, wbf_ref = refs[pos], refs[pos + 1]

    @pl.when(pl.program_id(1) == 0)
    def _():
        wbf_ref[...] = w_ref[...].astype(BF16)

    acc = jnp.dot(x_ref[...].astype(BF16), wbf_ref[...], preferred_element_type=F32)
    if has_bias:
        acc = acc + b_ref[...]
    if has_res:
        acc = acc + r_ref[...]
    o_ref[...] = acc


def matmul(x, w, layer, n_out, tm, tn, bias=None, res=None):
    m, k = x.shape
    in_specs = [pl.BlockSpec((tm, k), lambda j, i: (i, 0)),
                pl.BlockSpec((None, k, tn), lambda j, i: (layer, 0, j))]
    args = [x, w]
    if bias is not None:
        in_specs.append(pl.BlockSpec((1, tn), lambda j, i: (0, j)))
        args.append(bias.reshape(1, n_out))
    if res is not None:
        in_specs.append(pl.BlockSpec((tm, tn), lambda j, i: (i, j)))
        args.append(res)
    return pl.pallas_call(
        functools.partial(_matmul_kernel, has_bias=bias is not None, has_res=res is not None),
        grid=(n_out // tn, m // tm),
        in_specs=in_specs,
        out_specs=pl.BlockSpec((tm, tn), lambda j, i: (i, j)),
        out_shape=jax.ShapeDtypeStruct((m, n_out), F32),
        scratch_shapes=[pltpu.VMEM((k, tn), BF16)],
        compiler_params=_params(("arbitrary", "arbitrary")),
        name="matmul",
    )(*args)


def _diff_lambda(lam_ref, lam_init):
    lp = lam_ref[...]
    a = jnp.sum(lp[0:1] * lp[1:2], axis=-1, keepdims=True)
    b = jnp.sum(lp[2:3] * lp[3:4], axis=-1, keepdims=True)
    return jnp.exp(a) - jnp.exp(b) + lam_init


def _subln(o, sub_ref, lam_init):
    ms = jnp.mean(o * o, axis=-1, keepdims=True)
    return (o * lax.rsqrt(ms + EPS) * sub_ref[...]) * (1.0 - lam_init)


def _diff_prompt_kernel(slopes_ref, q_ref, k_ref, v_ref, lam_ref, sub_ref, o_ref,
                        qs_ref, m_ref, l_ref, acc_ref, *, lam_init):
    h = pl.program_id(1)
    i = pl.program_id(2)
    rows = 4 * BLK
    lane = lax.broadcasted_iota(jnp.int32, (BLK, BLK), 1)
    for g in range(2):
        qg = q_ref[0, :, g * BLK:(g + 1) * BLK] * (A_DH ** -0.5)
        qs_ref[(2 * g) * BLK:(2 * g + 1) * BLK, :] = jnp.where(lane < A_DH, qg, 0.0).astype(BF16)
        qs_ref[(2 * g + 1) * BLK:(2 * g + 2) * BLK, :] = jnp.where(lane >= A_DH, qg, 0.0).astype(BF16)
    m_ref[...] = jnp.full(m_ref.shape, NEG_INF, F32)
    l_ref[...] = jnp.zeros(l_ref.shape, F32)
    acc_ref[...] = jnp.zeros(acc_ref.shape, F32)

    r_io = lax.broadcasted_iota(jnp.int32, (rows, BLK), 0)
    c_io = lax.broadcasted_iota(jnp.int32, (rows, BLK), 1)
    rel = ((r_io & (BLK - 1)) - c_io).astype(F32)
    r_col = lax.broadcasted_iota(jnp.int32, (rows, 1), 0)
    slope = jnp.where(r_col < 2 * BLK, slopes_ref[2 * h], slopes_ref[2 * h + 1])

    def step(j, masked):
        start = pl.multiple_of(j * BLK, BLK)
        kj = k_ref[0, pl.ds(start, BLK), :].astype(BF16)
        vj = v_ref[0, pl.ds(start, BLK), :].astype(BF16)
        s = lax.dot_general(qs_ref[...], kj, (((1,), (1,)), ((), ())), preferred_element_type=F32)
        dist = rel + ((i - j) * BLK).astype(F32)
        s = s - slope * dist
        if masked:
            s = jnp.where(dist >= 0, s, NEG_INF)
        m_old = m_ref[...]
        m_new = jnp.maximum(m_old, jnp.max(s, axis=-1, keepdims=True))
        alpha = jnp.exp(m_old - m_new)
        p = jnp.exp(s - m_new)
        l_ref[...] = alpha * l_ref[...] + jnp.sum(p, axis=-1, keepdims=True)
        acc_ref[...] = alpha * acc_ref[...] + jnp.dot(p.astype(BF16), vj, preferred_element_type=F32)
        m_ref[...] = m_new

    def body(j, carry):
        step(j, False)
        return carry

    lax.fori_loop(0, i, body, 0)
    step(i, True)

    lam = _diff_lambda(lam_ref, lam_init)
    o = acc_ref[...] / l_ref[...]
    for g in range(2):
        og = o[(2 * g) * BLK:(2 * g + 1) * BLK] - lam * o[(2 * g + 1) * BLK:(2 * g + 2) * BLK]
        o_ref[0, :, g * BLK:(g + 1) * BLK] = _subln(og, sub_ref, lam_init).astype(o_ref.dtype)


def diff_prompt(qkv, lam_p, subln, slopes, lam_init):
    b = qkv.shape[0]
    kern = functools.partial(_diff_prompt_kernel, lam_init=lam_init)
    return pl.pallas_call(
        kern,
        grid=(b, A_KV, N_BLK),
        in_specs=[
            pl.BlockSpec(memory_space=pltpu.SMEM),
            pl.BlockSpec((1, BLK, 2 * BLK), lambda bb, h, i: (bb, i, h)),
            pl.BlockSpec((1, LP, BLK), lambda bb, h, i: (bb, 0, 16 + h)),
            pl.BlockSpec((1, LP, BLK), lambda bb, h, i: (bb, 0, 24 + h)),
            pl.BlockSpec((4, A_DH), lambda bb, h, i: (0, 0)),
            pl.BlockSpec((1, BLK), lambda bb, h, i: (0, 0)),
        ],
        out_specs=pl.BlockSpec((1, BLK, 2 * BLK), lambda bb, h, i: (bb, i, h)),
        out_shape=jax.ShapeDtypeStruct((b, LP, 2 * A_KV * BLK), BF16),
        scratch_shapes=[pltpu.VMEM((4 * BLK, BLK), BF16),
                        pltpu.VMEM((4 * BLK, 1), F32),
                        pltpu.VMEM((4 * BLK, 1), F32),
                        pltpu.VMEM((4 * BLK, BLK), F32)],
        compiler_params=_params(("parallel", "parallel", "arbitrary")),
        name="diff_prompt",
    )(slopes, qkv, qkv, qkv, lam_p, subln.reshape(1, BLK))


PAGES_PER_STEP = 4


def _diff_decode_kernel(*refs, lam_init, past):
    pt_ref, slopes_ref, q_ref, kn_ref, vn_ref = refs[:5]
    kp = refs[5:5 + PAGES_PER_STEP]
    vp = refs[5 + PAGES_PER_STEP:5 + 2 * PAGES_PER_STEP]
    lam_ref, sub_ref, o_ref, qs_ref, m_ref, l_ref, acc_ref, knp_ref, vnp_ref = refs[5 + 2 * PAGES_PER_STEP:]
    del pt_ref
    c = pl.program_id(1)
    n_c = pl.num_programs(1)
    chunk = PAGES_PER_STEP * PAGE
    lane8 = lax.broadcasted_iota(jnp.int32, (8, BLK), 1)
    r16 = lax.broadcasted_iota(jnp.int32, (16, 1), 0)
    t_row = (r16 & 3).astype(F32)
    g_row = (r16 >> 2) & 1

    @pl.when(c == 0)
    def _():
        for h in range(A_KV):
            q8 = q_ref[0, h] * (A_DH ** -0.5)
            qs_ref[h, 0:8, :] = jnp.where(lane8 < A_DH, q8, 0.0).astype(BF16)
            qs_ref[h, 8:16, :] = jnp.where(lane8 >= A_DH, q8, 0.0).astype(BF16)
        m_ref[...] = jnp.full(m_ref.shape, NEG_INF, F32)
        l_ref[...] = jnp.zeros(l_ref.shape, F32)
        acc_ref[...] = jnp.zeros(acc_ref.shape, F32)

    def update(h, s, v_list):
        m_old = m_ref[h]
        m_new = jnp.maximum(m_old, jnp.max(s, axis=-1, keepdims=True))
        alpha = jnp.exp(m_old - m_new)
        p = jnp.exp(s - m_new)
        l_ref[h] = alpha * l_ref[h] + jnp.sum(p, axis=-1, keepdims=True)
        pv = None
        for n, vh in enumerate(v_list):
            part = jnp.dot(p[:, n * PAGE:(n + 1) * PAGE].astype(BF16), vh, preferred_element_type=F32)
            pv = part if pv is None else pv + part
        acc_ref[h] = alpha * acc_ref[h] + pv
        m_ref[h] = m_new

    col = lax.broadcasted_iota(jnp.int32, (16, chunk), 1)
    for h in range(A_KV):
        slope = jnp.where(g_row == 0, slopes_ref[2 * h], slopes_ref[2 * h + 1])
        q16 = qs_ref[h]
        s_parts = []
        v_list = []
        for n in range(PAGES_PER_STEP):
            kh = kp[n][pl.ds(h, PAGE, stride=A_KV), :].astype(BF16)
            v_list.append(vp[n][pl.ds(h, PAGE, stride=A_KV), :].astype(BF16))
            s_parts.append(lax.dot_general(q16, kh, (((1,), (1,)), ((), ())), preferred_element_type=F32))
        s = jnp.concatenate(s_parts, axis=-1)
        dist = (t_row + float(past)) - (c * chunk + col).astype(F32)
        update(h, s - slope * dist, v_list)

    @pl.when(c == n_c - 1)
    def _():
        knp_ref[...] = jnp.zeros(knp_ref.shape, F32)
        vnp_ref[...] = jnp.zeros(vnp_ref.shape, F32)
        knp_ref[0:DEC_SEQ, :] = kn_ref[0]
        vnp_ref[0:DEC_SEQ, :] = vn_ref[0]
        coln = lax.broadcasted_iota(jnp.int32, (16, PAGE), 1).astype(F32)
        distn = t_row - coln
        lam = _diff_lambda(lam_ref, lam_init)
        for h in range(A_KV):
            slope = jnp.where(g_row == 0, slopes_ref[2 * h], slopes_ref[2 * h + 1])
            kh = knp_ref[:, h * BLK:(h + 1) * BLK].astype(BF16)
            vh = vnp_ref[:, h * BLK:(h + 1) * BLK].astype(BF16)
            s = lax.dot_general(qs_ref[h], kh, (((1,), (1,)), ((), ())), preferred_element_type=F32)
            s = jnp.where(distn >= 0, s - slope * distn, NEG_INF)
            update(h, s, [vh])
            o = acc_ref[h] / l_ref[h]
            og = o[0:8] - lam * o[8:16]
            o_ref[0, h] = _subln(og, sub_ref, lam_init)


def diff_decode(q_r, k_new, v_new, cache_k, cache_v, page_table, layer, lam_p, subln, slopes, lam_init):
    bd = q_r.shape[0]
    n_pages = page_table.shape[1]
    past = n_pages * PAGE
    n_c = n_pages // PAGES_PER_STEP

    def page_spec(n):
        return pl.BlockSpec((None, None, PAGE * A_KV, BLK),
                            lambda b, c, pt: (layer, pt[b, c * PAGES_PER_STEP + n], 0, 0))

    in_specs = [
        pl.BlockSpec(memory_space=pltpu.SMEM),
        pl.BlockSpec((1, A_KV, 8, BLK), lambda b, c, pt: (b, 0, 0, 0)),
        pl.BlockSpec((1, DEC_SEQ, A_KV * BLK), lambda b, c, pt: (b, 0, 0)),
        pl.BlockSpec((1, DEC_SEQ, A_KV * BLK), lambda b, c, pt: (b, 0, 0)),
    ]
    in_specs += [page_spec(n) for n in range(PAGES_PER_STEP)]
    in_specs += [page_spec(n) for n in range(PAGES_PER_STEP)]
    in_specs += [pl.BlockSpec((4, A_DH), lambda b, c, pt: (0, 0)),
                 pl.BlockSpec((1, BLK), lambda b, c, pt: (0, 0))]
    kern = functools.partial(_diff_decode_kernel, lam_init=lam_init, past=past)
    return pl.pallas_call(
        kern,
        grid_spec=pltpu.PrefetchScalarGridSpec(
            num_scalar_prefetch=1,
            grid=(bd, n_c),
            in_specs=in_specs,
            out_specs=pl.BlockSpec((1, A_KV, 8, BLK), lambda b, c, pt: (b, 0, 0, 0)),
            scratch_shapes=[pltpu.VMEM((A_KV, 16, BLK), BF16),
                            pltpu.VMEM((A_KV, 16, 1), F32),
                            pltpu.VMEM((A_KV, 16, 1), F32),
                            pltpu.VMEM((A_KV, 16, BLK), F32),
                            pltpu.VMEM((PAGE, A_KV * BLK), F32),
                            pltpu.VMEM((PAGE, A_KV * BLK), F32)],
        ),
        out_shape=jax.ShapeDtypeStruct((bd, A_KV, 8, BLK), F32),
        compiler_params=_params(("parallel", "arbitrary")),
        name="diff_decode",
    )(page_table, slopes, q_r, k_new, v_new,
      *([cache_k] * PAGES_PER_STEP), *([cache_v] * PAGES_PER_STEP), lam_p, subln.reshape(1, BLK))


def _swa_head(qh, kh, vh, dist, valid, slope, sink):
    s = lax.dot_general(qh, kh, (((1,), (1,)), ((), ())), preferred_element_type=F32)
    s = jnp.where(valid, s - slope * dist, NEG_INF)
    m = jnp.maximum(jnp.max(s, axis=-1, keepdims=True), sink)
    p = jnp.exp(s - m)
    p = p / (jnp.sum(p, axis=-1, keepdims=True) + jnp.exp(sink - m))
    return jnp.dot(p.astype(BF16), vh, preferred_element_type=F32)


def _swa_prompt_kernel(slopes_ref, sinks_ref, q_ref, kp_ref, kc_ref, vp_ref, vc_ref, o_ref):
    i = pl.program_id(1)
    kk = jnp.concatenate([kp_ref[0], kc_ref[0]], axis=0).astype(BF16)
    vv = jnp.concatenate([vp_ref[0], vc_ref[0]], axis=0).astype(BF16)
    row = lax.broadcasted_iota(jnp.int32, (BLK, 2 * BLK), 0)
    col = lax.broadcasted_iota(jnp.int32, (BLK, 2 * BLK), 1)
    dist_i = row + BLK - col
    k_pos = (i - 1) * BLK + col
    valid = (dist_i >= 0) & (dist_i <= WINDOW) & (k_pos >= 0)
    dist = dist_i.astype(F32)
    for pair in range(B_HEADS // 2):
        outs = []
        for hh in (2 * pair, 2 * pair + 1):
            kv = hh // B_GROUP
            qh = (q_ref[0, :, hh * B_DH:(hh + 1) * B_DH] * (B_DH ** -0.5)).astype(BF16)
            kh = kk[:, kv * B_DH:(kv + 1) * B_DH]
            vh = vv[:, kv * B_DH:(kv + 1) * B_DH]
            outs.append(_swa_head(qh, kh, vh, dist, valid, slopes_ref[hh], sinks_ref[hh]))
        o_ref[0, :, pair * BLK:(pair + 1) * BLK] = jnp.concatenate(outs, axis=-1).astype(o_ref.dtype)


def swa_prompt(qkv, sinks, slopes):
    b = qkv.shape[0]
    kvw = B_KV * B_DH

    def prev(bb, i):
        return jnp.maximum(i - 1, 0)

    return pl.pallas_call(
        _swa_prompt_kernel,
        grid=(b, N_BLK),
        in_specs=[
            pl.BlockSpec(memory_space=pltpu.SMEM),
            pl.BlockSpec(memory_space=pltpu.SMEM),
            pl.BlockSpec((1, BLK, D_MODEL), lambda bb, i: (bb, i, 0)),
            pl.BlockSpec((1, BLK, kvw), lambda bb, i: (bb, prev(bb, i), 4)),
            pl.BlockSpec((1, BLK, kvw), lambda bb, i: (bb, i, 4)),
            pl.BlockSpec((1, BLK, kvw), lambda bb, i: (bb, prev(bb, i), 5)),
            pl.BlockSpec((1, BLK, kvw), lambda bb, i: (bb, i, 5)),
        ],
        out_specs=pl.BlockSpec((1, BLK, D_MODEL), lambda bb, i: (bb, i, 0)),
        out_shape=jax.ShapeDtypeStruct((b, LP, D_MODEL), BF16),
        compiler_params=_params(("parallel", "parallel")),
        name="swa_prompt",
    )(slopes, sinks, qkv, qkv, qkv, qkv, qkv)


KPAD = 2 * WINDOW


def _swa_decode_kernel(slopes_ref, sinks_ref, q_ref, kb_ref, vb_ref, kn_ref, vn_ref,
                       o_ref, ko_ref, vo_ref, kf_ref, vf_ref):
    for buf, new, full, out in ((kb_ref, kn_ref, kf_ref, ko_ref), (vb_ref, vn_ref, vf_ref, vo_ref)):
        full[...] = jnp.zeros(full.shape, F32)
        full[0:WINDOW, :] = buf[0]
        full[WINDOW:WINDOW + DEC_SEQ, :] = new[0]
        out[0, 0:WINDOW - DEC_SEQ, :] = buf[0, DEC_SEQ:WINDOW, :]
        out[0, WINDOW - DEC_SEQ:WINDOW, :] = new[0]
    rows = B_GROUP * DEC_SEQ
    row = lax.broadcasted_iota(jnp.int32, (rows, KPAD), 0)
    col = lax.broadcasted_iota(jnp.int32, (rows, KPAD), 1)
    dist_i = (row & (DEC_SEQ - 1)) + WINDOW - col
    valid = (dist_i >= 0) & (dist_i <= WINDOW) & (col < WINDOW + DEC_SEQ)
    dist = dist_i.astype(F32)
    g_col = lax.broadcasted_iota(jnp.int32, (rows, 1), 0) >> 2
    for kv in range(B_KV):
        slope = jnp.zeros((rows, 1), F32)
        sink = jnp.zeros((rows, 1), F32)
        for g in range(B_GROUP):
            slope = jnp.where(g_col == g, slopes_ref[kv * B_GROUP + g], slope)
            sink = jnp.where(g_col == g, sinks_ref[kv * B_GROUP + g], sink)
        qh = (q_ref[0, kv] * (B_DH ** -0.5)).astype(BF16)
        kh = kf_ref[:, kv * B_DH:(kv + 1) * B_DH].astype(BF16)
        vh = vf_ref[:, kv * B_DH:(kv + 1) * B_DH].astype(BF16)
        o_ref[0, kv] = _swa_head(qh, kh, vh, dist, valid, slope, sink)


def swa_decode(q_r, k_buf, v_buf, k_new, v_new, sinks, slopes):
    bd = q_r.shape[0]
    kvw = B_KV * B_DH
    buf_spec = pl.BlockSpec((1, WINDOW, kvw), lambda b: (b, 0, 0))
    new_spec = pl.BlockSpec((1, DEC_SEQ, kvw), lambda b: (b, 0, 0))
    return pl.pallas_call(
        _swa_decode_kernel,
        grid=(bd,),
        in_specs=[
            pl.BlockSpec(memory_space=pltpu.SMEM),
            pl.BlockSpec(memory_space=pltpu.SMEM),
            pl.BlockSpec((1, B_KV, B_GROUP * DEC_SEQ, B_DH), lambda b: (b, 0, 0, 0)),
            buf_spec, buf_spec, new_spec, new_spec,
        ],
        out_specs=[pl.BlockSpec((1, B_KV, B_GROUP * DEC_SEQ, B_DH), lambda b: (b, 0, 0, 0)),
                   buf_spec, buf_spec],
        out_shape=[jax.ShapeDtypeStruct((bd, B_KV, B_GROUP * DEC_SEQ, B_DH), F32),
                   jax.ShapeDtypeStruct((bd, WINDOW, kvw), F32),
                   jax.ShapeDtypeStruct((bd, WINDOW, kvw), F32)],
        scratch_shapes=[pltpu.VMEM((KPAD, kvw), F32), pltpu.VMEM((KPAD, kvw), F32)],
        compiler_params=_params(("parallel",)),
        name="swa_decode",
    )(slopes, sinks, q_r, k_buf, v_buf, k_new, v_new)


def _silu(x):
    return x * (1.0 / (1.0 + jnp.exp(-x)))


def _ffn_tail(gate_prev2, gate_prev1, gate, up, cw_ref, cb_ref, wd_ref, h_ref, o_ref):
    cw = cw_ref[...]
    gc = cb_ref[...] + (cw[0:1] * gate_prev2 + cw[1:2] * gate_prev1 + cw[2:3] * gate)
    act = (_silu(gc) * up).astype(BF16)
    part = jnp.dot(act, wd_ref[...].astype(BF16), preferred_element_type=F32)

    @pl.when(pl.program_id(1) == 0)
    def _():
        o_ref[...] = h_ref[...] + part

    @pl.when(pl.program_id(1) > 0)
    def _():
        o_ref[...] += part


def _ffn_prompt_kernel(xn_ref, h_ref, wg_ref, wu_ref, cw_ref, cb_ref, wd_ref, o_ref, st_ref, tail_ref,
                       *, tiles_per_seq, state_row):
    i = pl.program_id(0)
    f = pl.program_id(1)
    xn = xn_ref[...]
    gate = jnp.dot(xn, wg_ref[...].astype(BF16), preferred_element_type=F32)
    up = jnp.dot(xn, wu_ref[...].astype(BF16), preferred_element_type=F32)
    tm = gate.shape[0]
    @pl.when(i % tiles_per_seq == 0)
    def _():
        tail_ref[f] = jnp.zeros(tail_ref.shape[1:], F32)

    prev = tail_ref[f]
    p1 = prev[7:8]
    p2 = prev[6:7]
    row = lax.broadcasted_iota(jnp.int32, (tm, 1), 0)
    g1 = jnp.where(row >= 1, pltpu.roll(gate, 1, 0), p1)
    g2 = jnp.where(row >= 2, pltpu.roll(gate, 2, 0), jnp.where(row == 1, p1, p2))
    tail_ref[f] = gate[tm - 8:tm]
    st_ref[0] = gate[state_row:state_row + 8]
    _ffn_tail(g2, g1, gate, up, cw_ref, cb_ref, wd_ref, h_ref, o_ref)


def _ffn_sample_kernel(xn_ref, h_ref, wg_ref, wu_ref, cw_ref, cb_ref, wd_ref, p1_ref, p2_ref, o_ref, g_ref):
    xn = xn_ref[...]
    gate = jnp.dot(xn, wg_ref[...].astype(BF16), preferred_element_type=F32)
    up = jnp.dot(xn, wu_ref[...].astype(BF16), preferred_element_type=F32)
    tm = gate.shape[0]
    t = lax.broadcasted_iota(jnp.int32, (tm, 1), 0) & (DEC_SEQ - 1)
    g1 = jnp.where(t >= 1, pltpu.roll(gate, 1, 0), p1_ref[...])
    g2 = jnp.where(t >= 2, pltpu.roll(gate, 2, 0), p2_ref[...])
    g_ref[...] = gate
    _ffn_tail(g2, g1, gate, up, cw_ref, cb_ref, wd_ref, h_ref, o_ref)


def _ffn_common_specs(layer, tm, tf):
    nf = D_FF // tf
    return [
        pl.BlockSpec((tm, D_MODEL), lambda i, f: (i, 0), pipeline_mode=pl.Buffered(1)),
        pl.BlockSpec((tm, D_MODEL), lambda i, f: (i, 0), pipeline_mode=pl.Buffered(1)),
        pl.BlockSpec((None, D_MODEL, tf), lambda i, f: (layer, 0, f)),
        pl.BlockSpec((None, D_MODEL, tf), lambda i, f: (layer, 0, nf + f)),
        pl.BlockSpec((None, 3, tf), lambda i, f: (layer, 0, f)),
        pl.BlockSpec((None, 1, tf), lambda i, f: (layer, 0, f)),
        pl.BlockSpec((None, tf, D_MODEL), lambda i, f: (layer, f, 0)),
    ]


def ffn_prompt(xn, h, w_gu, conv_w, conv_b, w_down, layer, tm, tf):
    m = xn.shape[0]
    n_tiles = m // tm
    tiles_per_seq = LP // tm
    state_row = ((L_REAL - 2) % tm) // 8 * 8
    kern = functools.partial(_ffn_prompt_kernel, tiles_per_seq=tiles_per_seq, state_row=state_row)
    return pl.pallas_call(
        kern,
        grid=(n_tiles, D_FF // tf),
        in_specs=_ffn_common_specs(layer, tm, tf),
        out_specs=[pl.BlockSpec((tm, D_MODEL), lambda i, f: (i, 0)),
                   pl.BlockSpec((1, 8, tf), lambda i, f: (i, 0, f))],
        out_shape=[jax.ShapeDtypeStruct((m, D_MODEL), F32),
                   jax.ShapeDtypeStruct((n_tiles, 8, D_FF), F32)],
        scratch_shapes=[pltpu.VMEM((D_FF // tf, 8, tf), F32)],
        compiler_params=_params(("arbitrary", "arbitrary")),
        name="ffn_prompt",
    )(xn, h, w_gu, w_gu, conv_w, conv_b.reshape(DEPTH, 1, D_FF), w_down)


def ffn_sample(xn, h, w_gu, conv_w, conv_b, w_down, layer, p1, p2, tf):
    m = xn.shape[0]
    specs = _ffn_common_specs(layer, m, tf)
    specs += [pl.BlockSpec((m, tf), lambda i, f: (0, f)), pl.BlockSpec((m, tf), lambda i, f: (0, f))]
    return pl.pallas_call(
        _ffn_sample_kernel,
        grid=(1, D_FF // tf),
        in_specs=specs,
        out_specs=[pl.BlockSpec((m, D_MODEL), lambda i, f: (0, 0)),
                   pl.BlockSpec((m, tf), lambda i, f: (0, f))],
        out_shape=[jax.ShapeDtypeStruct((m, D_MODEL), F32),
                   jax.ShapeDtypeStruct((m, D_FF), F32)],
        compiler_params=_params(("arbitrary", "arbitrary")),
        name="ffn_sample",
    )(xn, h, w_gu, w_gu, conv_w, conv_b.reshape(DEPTH, 1, D_FF), w_down, p1, p2)


TM_PROMPT = LP // 2
TM_NORM = LP // 4
TN = 512
TF = 256


def _alibi(n):
    return jnp.asarray(np.power(2.0, -8.0 * np.arange(1, n + 1, dtype=np.float32) / n).astype(np.float32))


def kernel(x_prompt, x_sample, cache_a_k, cache_a_v, state_b_k, state_b_v, state_conv, page_table, meta,
           norm_mix, norm_ffn, norm_final, w_qkv_a, lambda_a, subln_a, w_o_a, w_qkv_b, b_qkv_b, sinks_b,
           w_o_b, b_o_b, w_gu, conv_w, conv_b, w_down):
    bp, bd = BATCH, DEC_BATCH
    slopes_a = _alibi(A_HEADS)
    slopes_b = _alibi(B_HEADS)
    n_pool = cache_a_k.shape[1]
    cache_k = cache_a_k.reshape(cache_a_k.shape[0], n_pool, PAGE * A_KV, 2 * A_DH)
    cache_v = cache_a_v.reshape(cache_a_v.shape[0], n_pool, PAGE * A_KV, 2 * A_DH)

    hp = jnp.concatenate([jnp.broadcast_to(meta[None], (bp, N_META, D_MODEL)), x_prompt,
                          jnp.zeros((bp, LP - L_REAL, D_MODEL), F32)], axis=1).reshape(M_PROMPT, D_MODEL)
    hs = x_sample.reshape(M_SAMPLE, D_MODEL)
    ka_p, va_p, ka_s, va_s = [], [], [], []
    kb_p, vb_p, kb_s, vb_s = [], [], [], []
    cv_p, cv_s = [], []
    for i in range(DEPTH):
        j = i // 2
        xnp = rmsnorm(hp, norm_mix[i], TM_NORM, BF16)
        xns = rmsnorm(hs, norm_mix[i], M_SAMPLE, BF16)
        if i % 2 == 0:
            lam_init = 0.8 - 0.6 * math.exp(-0.3 * i)
            n_qkv = w_qkv_a.shape[2]
            qkv_p = matmul(xnp, w_qkv_a, j, n_qkv, TM_PROMPT, TN).reshape(bp, LP, n_qkv)
            o_p = diff_prompt(qkv_p, lambda_a[j], subln_a[j], slopes_a, lam_init)
            hp = matmul(o_p.reshape(M_PROMPT, D_MODEL), w_o_a, j, D_MODEL, TM_PROMPT, TN, res=hp)
            ka_p.append(qkv_p[:, :L_REAL, 2048:3072].reshape(bp, L_REAL, A_KV, 2 * A_DH))
            va_p.append(qkv_p[:, :L_REAL, 3072:4096].reshape(bp, L_REAL, A_KV, 2 * A_DH))

            qkv_s = matmul(xns, w_qkv_a, j, n_qkv, M_SAMPLE, TN).reshape(bd, DEC_SEQ, n_qkv)
            q_r = qkv_s[:, :, :2048].reshape(bd, DEC_SEQ, A_KV, 2, 2 * A_DH)
            q_r = q_r.transpose(0, 2, 3, 1, 4).reshape(bd, A_KV, 2 * DEC_SEQ, 2 * A_DH)
            k_new = qkv_s[:, :, 2048:3072]
            v_new = qkv_s[:, :, 3072:4096]
            o_s = diff_decode(q_r, k_new, v_new, cache_k, cache_v, page_table, j,
                              lambda_a[j], subln_a[j], slopes_a, lam_init)
            o_s = o_s.reshape(bd, A_KV, 2, DEC_SEQ, 2 * A_DH).transpose(0, 3, 1, 2, 4).reshape(M_SAMPLE, D_MODEL)
            hs = matmul(o_s, w_o_a, j, D_MODEL, M_SAMPLE, TN, res=hs)
            ka_s.append(k_new.reshape(bd, DEC_SEQ, A_KV, 2 * A_DH))
            va_s.append(v_new.reshape(bd, DEC_SEQ, A_KV, 2 * A_DH))
        else:
            n_qkv = w_qkv_b.shape[2]
            kvw = B_KV * B_DH
            qkv_p = matmul(xnp, w_qkv_b, j, n_qkv, TM_PROMPT, TN, bias=b_qkv_b[j]).reshape(bp, LP, n_qkv)
            o_p = swa_prompt(qkv_p, sinks_b[j], slopes_b)
            hp = matmul(o_p.reshape(M_PROMPT, D_MODEL), w_o_b, j, D_MODEL, TM_PROMPT, TN, bias=b_o_b[j], res=hp)
            kb_p.append(qkv_p[:, L_REAL - WINDOW:L_REAL, 2048:2048 + kvw].reshape(bp, WINDOW, B_KV, B_DH))
            vb_p.append(qkv_p[:, L_REAL - WINDOW:L_REAL, 2048 + kvw:].reshape(bp, WINDOW, B_KV, B_DH))

            qkv_s = matmul(xns, w_qkv_b, j, n_qkv, M_SAMPLE, TN, bias=b_qkv_b[j]).reshape(bd, DEC_SEQ, n_qkv)
            q_r = qkv_s[:, :, :2048].reshape(bd, DEC_SEQ, B_KV, B_GROUP, B_DH)
            q_r = q_r.transpose(0, 2, 3, 1, 4).reshape(bd, B_KV, B_GROUP * DEC_SEQ, B_DH)
            o_s, k_st, v_st = swa_decode(q_r, state_b_k[j].reshape(bd, WINDOW, kvw),
                                         state_b_v[j].reshape(bd, WINDOW, kvw),
                                         qkv_s[:, :, 2048:2048 + kvw], qkv_s[:, :, 2048 + kvw:],
                                         sinks_b[j], slopes_b)
            o_s = o_s.reshape(bd, B_KV, B_GROUP, DEC_SEQ, B_DH).transpose(0, 3, 1, 2, 4).reshape(M_SAMPLE, D_MODEL)
            hs = matmul(o_s, w_o_b, j, D_MODEL, M_SAMPLE, TN, bias=b_o_b[j], res=hs)
            kb_s.append(k_st.reshape(bd, WINDOW, B_KV, B_DH))
            vb_s.append(v_st.reshape(bd, WINDOW, B_KV, B_DH))

        xnp = rmsnorm(hp, norm_ffn[i], TM_NORM, BF16)
        hp, tails = ffn_prompt(xnp, hp, w_gu, conv_w, conv_b, w_down, i, TM_PROMPT, TF)
        off = (L_REAL - 2) % TM_PROMPT % 8
        cv_p.append(tails[LP // TM_PROMPT - 1::LP // TM_PROMPT, off:off + 2])

        xns = rmsnorm(hs, norm_ffn[i], M_SAMPLE, BF16)
        st = state_conv[i]
        zero = jnp.zeros((bd, 1, D_FF), F32)
        p1 = jnp.concatenate([st[:, 1:2], zero, zero, zero], axis=1).reshape(M_SAMPLE, D_FF)
        p2 = jnp.concatenate([st[:, 0:1], st[:, 1:2], zero, zero], axis=1).reshape(M_SAMPLE, D_FF)
        hs, gate_s = ffn_sample(xns, hs, w_gu, conv_w, conv_b, w_down, i, p1, p2, TF)
        cv_s.append(gate_s.reshape(bd, DEC_SEQ, D_FF)[:, DEC_SEQ - 2:])

    y_prompt = rmsnorm(hp, norm_final, TM_NORM, F32).reshape(bp, LP, D_MODEL)[:, N_META:L_REAL]
    y_sample = rmsnorm(hs, norm_final, M_SAMPLE, F32).reshape(bd, DEC_SEQ, D_MODEL)
    return (y_prompt, y_sample, jnp.stack(ka_p), jnp.stack(va_p), jnp.stack(ka_s), jnp.stack(va_s),
            jnp.stack(kb_p), jnp.stack(vb_p), jnp.stack(kb_s), jnp.stack(vb_s),
            jnp.stack(cv_p), jnp.stack(cv_s))
```

```python
import functools
import math

import numpy as np
import jax
import jax.numpy as jnp
from jax import lax
from jax.experimental import pallas as pl
from jax.experimental.pallas import tpu as pltpu

F32 = jnp.float32
BF16 = jnp.bfloat16

D_MODEL = 2048
BATCH = 4
SEQ = 2048
DEPTH = 4
DEC_BATCH = 32
DEC_SEQ = 4
PAGE = 128
N_META = 16
EPS = 1e-6
NEG_INF = -1e30
A_HEADS = 16
A_KV = 8
A_DH = 64
B_HEADS = 32
B_KV = 8
B_GROUP = 4
B_DH = 64
WINDOW = 128
D_FF = 5632

L_REAL = N_META + SEQ
BLK = 128
LP = -(-L_REAL // BLK) * BLK
N_BLK = LP // BLK
M_PROMPT = BATCH * LP
M_SAMPLE = DEC_BATCH * DEC_SEQ

VMEM_LIMIT = 56 * 1024 * 1024


def _params(sem, vmem=VMEM_LIMIT):
    return pltpu.CompilerParams(dimension_semantics=sem, vmem_limit_bytes=vmem)


def _rmsnorm_kernel(x_ref, g_ref, o_ref):
    x = x_ref[...]
    ms = jnp.mean(x * x, axis=-1, keepdims=True)
    o_ref[...] = (x * lax.rsqrt(ms + EPS) * g_ref[...]).astype(o_ref.dtype)


def rmsnorm(x, g, tm, out_dtype):
    m, d = x.shape
    return pl.pallas_call(
        _rmsnorm_kernel,
        grid=(m // tm,),
        in_specs=[pl.BlockSpec((tm, d), lambda i: (i, 0)),
                  pl.BlockSpec((1, d), lambda i: (0, 0))],
        out_specs=pl.BlockSpec((tm, d), lambda i: (i, 0)),
        out_shape=jax.ShapeDtypeStruct((m, d), out_dtype),
        compiler_params=_params(("parallel",)),
        name="rmsnorm",
    )(x, g.reshape(1, d))


def _matmul_kernel(*refs, has_bias, has_res):
    x_ref, w_ref = refs[0], refs[1]
    pos = 2
    b_ref = r_ref = None
    if has_bias:
        b_ref = refs[pos]
        pos += 1
    if has_res:
        r_ref = refs[pos]
        pos += 1
    o_ref = refs[pos]
    acc = jnp.dot(x_ref[...].astype(BF16), w_ref[...], preferred_element_type=F32)
    if has_bias:
        acc = acc + b_ref[...]
    if has_res:
        acc = acc + r_ref[...]
    o_ref[...] = acc


def matmul(x, w, layer, n_out, tm, tn, bias=None, res=None):
    m, k = x.shape
    in_specs = [pl.BlockSpec((tm, k), lambda j, i: (i, 0)),
                pl.BlockSpec((None, k, tn), lambda j, i: (layer, 0, j))]
    args = [x, w]
    if bias is not None:
        in_specs.append(pl.BlockSpec((1, tn), lambda j, i: (0, j)))
        args.append(bias.reshape(1, n_out))
    if res is not None:
        in_specs.append(pl.BlockSpec((tm, tn), lambda j, i: (i, j)))
        args.append(res)
    return pl.pallas_call(
        functools.partial(_matmul_kernel, has_bias=bias is not None, has_res=res is not None),
        grid=(n_out // tn, m // tm),
        in_specs=in_specs,
        out_specs=pl.BlockSpec((tm, tn), lambda j, i: (i, j)),
        out_shape=jax.ShapeDtypeStruct((m, n_out), F32),
        compiler_params=_params(("arbitrary", "arbitrary")),
        name="matmul",
    )(*args)


def _diff_lambda(lam_ref, lam_init):
    lp = lam_ref[...]
    a = jnp.sum(lp[0:1] * lp[1:2], axis=-1, keepdims=True)
    b = jnp.sum(lp[2:3] * lp[3:4], axis=-1, keepdims=True)
    return jnp.exp(a) - jnp.exp(b) + lam_init


def _subln(o, sub_ref, lam_init):
    ms = jnp.mean(o * o, axis=-1, keepdims=True)
    return (o * lax.rsqrt(ms + EPS) * sub_ref[...]) * (1.0 - lam_init)


def _diff_prompt_kernel(slopes_ref, q_ref, k_ref, v_ref, lam_ref, sub_ref, o_ref,
                        kb_ref, vt_ref, qt_ref, m_ref, l_ref, acc_ref, *, lam_init):
    h = pl.program_id(1)
    i = pl.program_id(2)

    @pl.when(i == 0)
    def _():
        for jj in range(N_BLK):
            kb_ref[jj] = k_ref[0, jj * BLK:(jj + 1) * BLK, :].astype(BF16)
            vt_ref[jj] = v_ref[0, jj * BLK:(jj + 1) * BLK, :].T.astype(BF16)

    d_io = lax.broadcasted_iota(jnp.int32, (BLK, BLK), 0)
    for g in range(2):
        qgt = (q_ref[0, :, g * BLK:(g + 1) * BLK] * (A_DH ** -0.5)).T
        qt_ref[:, (2 * g) * BLK:(2 * g + 1) * BLK] = jnp.where(d_io < A_DH, qgt, 0.0).astype(BF16)
        qt_ref[:, (2 * g + 1) * BLK:(2 * g + 2) * BLK] = jnp.where(d_io >= A_DH, qgt, 0.0).astype(BF16)
    m_ref[...] = jnp.full(m_ref.shape, NEG_INF, F32)
    l_ref[...] = jnp.zeros(l_ref.shape, F32)
    acc_ref[...] = jnp.zeros(acc_ref.shape, F32)

    k_io = lax.broadcasted_iota(jnp.int32, (BLK, 2 * BLK), 0)
    c_io = lax.broadcasted_iota(jnp.int32, (BLK, 2 * BLK), 1)
    rel = ((c_io & (BLK - 1)) - k_io).astype(F32)
    slope = [slopes_ref[2 * h + g] for g in range(2)]
    srel = [slope[g] * rel for g in range(2)]

    def step(j, masked):
        kj = kb_ref[j]
        vtj = vt_ref[j]
        off = ((i - j) * BLK).astype(F32)
        for g in range(2):
            cs = slice(2 * g * BLK, (2 * g + 2) * BLK)
            t = jnp.dot(kj, qt_ref[:, cs], preferred_element_type=F32) - srel[g]
            if masked:
                t = jnp.where(rel >= 0, t, NEG_INF)
            shift = slope[g] * off
            m_old = m_ref[:, cs]
            m_new = jnp.maximum(m_old, jnp.max(t, axis=0, keepdims=True) - shift)
            alpha = jnp.exp(m_old - m_new)
            p = jnp.exp(t - (m_new + shift))
            l_ref[:, cs] = alpha * l_ref[:, cs] + jnp.sum(p, axis=0, keepdims=True)
            acc_ref[:, cs] = alpha * acc_ref[:, cs] + jnp.dot(vtj, p.astype(BF16), preferred_element_type=F32)
            m_ref[:, cs] = m_new

    def group(start, count, mask_last):
        for u in range(count):
            step(start + u, mask_last and u == count - 1)

    n = i + 1
    nq = n // 4
    r = n - 4 * nq
    for rr in (1, 2, 3):
        @pl.when(jnp.logical_and(r == rr, nq == 0))
        def _(rr=rr):
            group(0, rr, True)

        @pl.when(jnp.logical_and(r == rr, nq > 0))
        def _(rr=rr):
            group(0, rr, False)

    def body(q, carry):
        group(r + 4 * q, 4, False)
        return carry

    lax.fori_loop(0, nq - 1, body, 0)

    @pl.when(nq > 0)
    def _():
        group(r + 4 * (nq - 1), 4, True)

    lam = _diff_lambda(lam_ref, lam_init)
    ot = acc_ref[...] / l_ref[...]
    for g in range(2):
        ogt = ot[:, (2 * g) * BLK:(2 * g + 1) * BLK] - lam * ot[:, (2 * g + 1) * BLK:(2 * g + 2) * BLK]
        o_ref[0, :, g * BLK:(g + 1) * BLK] = _subln(ogt.T, sub_ref, lam_init).astype(o_ref.dtype)


def diff_prompt(qkv, lam_p, subln, slopes, lam_init):
    b = qkv.shape[0]
    kern = functools.partial(_diff_prompt_kernel, lam_init=lam_init)
    return pl.pallas_call(
        kern,
        grid=(b, A_KV, N_BLK),
        in_specs=[
            pl.BlockSpec(memory_space=pltpu.SMEM),
            pl.BlockSpec((1, BLK, 2 * BLK), lambda bb, h, i: (bb, i, h)),
            pl.BlockSpec((1, LP, BLK), lambda bb, h, i: (bb, 0, 16 + h)),
            pl.BlockSpec((1, LP, BLK), lambda bb, h, i: (bb, 0, 24 + h)),
            pl.BlockSpec((4, A_DH), lambda bb, h, i: (0, 0)),
            pl.BlockSpec((1, BLK), lambda bb, h, i: (0, 0)),
        ],
        out_specs=pl.BlockSpec((1, BLK, 2 * BLK), lambda bb, h, i: (bb, i, h)),
        out_shape=jax.ShapeDtypeStruct((b, LP, 2 * A_KV * BLK), BF16),
        scratch_shapes=[pltpu.VMEM((N_BLK, BLK, BLK), BF16),
                        pltpu.VMEM((N_BLK, BLK, BLK), BF16),
                        pltpu.VMEM((BLK, 4 * BLK), BF16),
                        pltpu.VMEM((1, 4 * BLK), F32),
                        pltpu.VMEM((1, 4 * BLK), F32),
                        pltpu.VMEM((BLK, 4 * BLK), F32)],
        compiler_params=_params(("arbitrary", "arbitrary", "arbitrary")),
        name="diff_prompt",
    )(slopes, qkv, qkv, qkv, lam_p, subln.reshape(1, BLK))


PAGES_PER_STEP = 8
DEC_ROWS = A_KV * 16


def _diff_decode_kernel(*refs, lam_init, past):
    pt_ref, q_ref, kn_ref, vn_ref, slope_ref, tq_ref = refs[:6]
    kp = refs[6:6 + PAGES_PER_STEP]
    vp = refs[6 + PAGES_PER_STEP:6 + 2 * PAGES_PER_STEP]
    lam_ref, sub_ref, o_ref, qs_ref, m_ref, l_ref, acc_ref, knp_ref, vnp_ref = refs[6 + 2 * PAGES_PER_STEP:]
    del pt_ref
    c = pl.program_id(1)
    n_c = pl.num_programs(1)
    chunk = PAGES_PER_STEP * PAGE
    nt = (((1,), (1,)), ((), ()))

    @pl.when(c == 0)
    def _():
        lane8 = lax.broadcasted_iota(jnp.int32, (8, BLK), 1)
        for h in range(A_KV):
            q8 = q_ref[0, h] * (A_DH ** -0.5)
            qs_ref[h, 0:8, :] = jnp.where(lane8 < A_DH, q8, 0.0).astype(BF16)
            qs_ref[h, 8:16, :] = jnp.where(lane8 >= A_DH, q8, 0.0).astype(BF16)
        m_ref[...] = jnp.full(m_ref.shape, NEG_INF, F32)
        l_ref[...] = jnp.zeros(l_ref.shape, F32)
        acc_ref[...] = jnp.zeros(acc_ref.shape, F32)

    def update(s_tiles, v_of):
        mx = s_tiles[0]
        for s in s_tiles[1:]:
            mx = jnp.maximum(mx, s)
        m_old = m_ref[...]
        m_new = jnp.maximum(m_old, jnp.max(mx, axis=-1, keepdims=True))
        alpha = jnp.exp(m_old - m_new)
        m_b = jnp.broadcast_to(m_new, (DEC_ROWS, BLK))
        ps = [jnp.exp(s - m_b) for s in s_tiles]
        psum = ps[0]
        for p in ps[1:]:
            psum = psum + p
        l_ref[...] = alpha * l_ref[...] + jnp.sum(psum, axis=-1, keepdims=True)
        pv_rows = []
        for h in range(A_KV):
            pv = None
            for n, p in enumerate(ps):
                part = jnp.dot(p[h * 16:(h + 1) * 16].astype(BF16), v_of(n, h), preferred_element_type=F32)
                pv = part if pv is None else pv + part
            pv_rows.append(pv)
        acc_ref[...] = alpha * acc_ref[...] + jnp.concatenate(pv_rows, axis=0)
        m_ref[...] = m_new

    slope = slope_ref[...]
    tq = tq_ref[...]
    kcol = lax.broadcasted_iota(jnp.int32, (1, PAGE), 1)
    s_tiles = []
    for n in range(PAGES_PER_STEP):
        rows = []
        for h in range(A_KV):
            kh = kp[n][pl.ds(h, PAGE, stride=A_KV), :].astype(BF16)
            rows.append(lax.dot_general(qs_ref[h], kh, nt, preferred_element_type=F32))
        k_pos = (c * chunk + n * PAGE + kcol).astype(F32)
        s_tiles.append(jnp.concatenate(rows, axis=0) - slope * (tq - k_pos))
    update(s_tiles, lambda n, h: vp[n][pl.ds(h, PAGE, stride=A_KV), :].astype(BF16))

    @pl.when(c == n_c - 1)
    def _():
        knp_ref[...] = jnp.zeros(knp_ref.shape, F32)
        vnp_ref[...] = jnp.zeros(vnp_ref.shape, F32)
        knp_ref[0:DEC_SEQ, :] = kn_ref[0]
        vnp_ref[0:DEC_SEQ, :] = vn_ref[0]
        rows = []
        for h in range(A_KV):
            kh = knp_ref[:, h * BLK:(h + 1) * BLK].astype(BF16)
            rows.append(lax.dot_general(qs_ref[h], kh, nt, preferred_element_type=F32))
        distn = tq - (float(past) + kcol.astype(F32))
        s_new = jnp.where(distn >= 0, jnp.concatenate(rows, axis=0) - slope * distn, NEG_INF)
        update([s_new], lambda n, h: vnp_ref[:, h * BLK:(h + 1) * BLK].astype(BF16))
        lam = _diff_lambda(lam_ref, lam_init)
        o = acc_ref[...] / l_ref[...]
        for h in range(A_KV):
            og = o[h * 16:h * 16 + 8] - lam * o[h * 16 + 8:h * 16 + 16]
            o_ref[0, h] = _subln(og, sub_ref, lam_init)


def diff_decode(q_r, k_new, v_new, cache_k, cache_v, page_table, layer, lam_p, subln, slopes, lam_init):
    bd = q_r.shape[0]
    n_pages = page_table.shape[1]
    past = n_pages * PAGE
    n_c = n_pages // PAGES_PER_STEP
    r = np.arange(DEC_ROWS)
    slope_rows = jnp.broadcast_to(slopes[2 * (r // 16) + (r // DEC_SEQ) % 2][:, None], (DEC_ROWS, BLK))
    tq_rows = jnp.asarray(np.broadcast_to((past + r % DEC_SEQ).astype(np.float32)[:, None], (DEC_ROWS, BLK)))

    def page_spec(n):
        return pl.BlockSpec((None, None, PAGE * A_KV, BLK),
                            lambda b, c, pt: (layer, pt[b, c * PAGES_PER_STEP + n], 0, 0))

    const_spec = pl.BlockSpec((DEC_ROWS, BLK), lambda b, c, pt: (0, 0))
    in_specs = [
        pl.BlockSpec((1, A_KV, 8, BLK), lambda b, c, pt: (b, 0, 0, 0)),
        pl.BlockSpec((1, DEC_SEQ, A_KV * BLK), lambda b, c, pt: (b, 0, 0)),
        pl.BlockSpec((1, DEC_SEQ, A_KV * BLK), lambda b, c, pt: (b, 0, 0)),
        const_spec, const_spec,
    ]
    in_specs += [page_spec(n) for n in range(PAGES_PER_STEP)]
    in_specs += [page_spec(n) for n in range(PAGES_PER_STEP)]
    in_specs += [pl.BlockSpec((4, A_DH), lambda b, c, pt: (0, 0)),
                 pl.BlockSpec((1, BLK), lambda b, c, pt: (0, 0))]
    kern = functools.partial(_diff_decode_kernel, lam_init=lam_init, past=past)
    return pl.pallas_call(
        kern,
        grid_spec=pltpu.PrefetchScalarGridSpec(
            num_scalar_prefetch=1,
            grid=(bd, n_c),
            in_specs=in_specs,
            out_specs=pl.BlockSpec((1, A_KV, 8, BLK), lambda b, c, pt: (b, 0, 0, 0)),
            scratch_shapes=[pltpu.VMEM((A_KV, 16, BLK), BF16),
                            pltpu.VMEM((DEC_ROWS, 1), F32),
                            pltpu.VMEM((DEC_ROWS, 1), F32),
                            pltpu.VMEM((DEC_ROWS, BLK), F32),
                            pltpu.VMEM((PAGE, A_KV * BLK), F32),
                            pltpu.VMEM((PAGE, A_KV * BLK), F32)],
        ),
        out_shape=jax.ShapeDtypeStruct((bd, A_KV, 8, BLK), F32),
        compiler_params=_params(("arbitrary", "arbitrary")),
        name="diff_decode",
    )(page_table, q_r, k_new, v_new, slope_rows, tq_rows,
      *([cache_k] * PAGES_PER_STEP), *([cache_v] * PAGES_PER_STEP), lam_p, subln.reshape(1, BLK))


def _swa_head(qh, kh, vh, dist, valid, slope, sink):
    s = lax.dot_general(qh, kh, (((1,), (1,)), ((), ())), preferred_element_type=F32)
    s = jnp.where(valid, s - slope * dist, NEG_INF)
    m = jnp.maximum(jnp.max(s, axis=-1, keepdims=True), sink)
    p = jnp.exp(s - m)
    p = p / (jnp.sum(p, axis=-1, keepdims=True) + jnp.exp(sink - m))
    return jnp.dot(p.astype(BF16), vh, preferred_element_type=F32)


def _swa_prompt_kernel(slopes_ref, sinks_ref, q_ref, kp_ref, kc_ref, vp_ref, vc_ref, o_ref):
    i = pl.program_id(1)
    kk = jnp.concatenate([kp_ref[0], kc_ref[0]], axis=0).astype(BF16)
    vv = jnp.concatenate([vp_ref[0], vc_ref[0]], axis=0).astype(BF16)
    row = lax.broadcasted_iota(jnp.int32, (BLK, 2 * BLK), 0)
    col = lax.broadcasted_iota(jnp.int32, (BLK, 2 * BLK), 1)
    dist_i = row + BLK - col
    k_pos = (i - 1) * BLK + col
    valid = (dist_i >= 0) & (dist_i <= WINDOW) & (k_pos >= 0)
    dist = dist_i.astype(F32)
    for pair in range(B_HEADS // 2):
        outs = []
        for hh in (2 * pair, 2 * pair + 1):
            kv = hh // B_GROUP
            qh = (q_ref[0, :, hh * B_DH:(hh + 1) * B_DH] * (B_DH ** -0.5)).astype(BF16)
            kh = kk[:, kv * B_DH:(kv + 1) * B_DH]
            vh = vv[:, kv * B_DH:(kv + 1) * B_DH]
            outs.append(_swa_head(qh, kh, vh, dist, valid, slopes_ref[hh], sinks_ref[hh]))
        o_ref[0, :, pair * BLK:(pair + 1) * BLK] = jnp.concatenate(outs, axis=-1).astype(o_ref.dtype)


def swa_prompt(qkv, sinks, slopes):
    b = qkv.shape[0]
    kvw = B_KV * B_DH

    def prev(bb, i):
        return jnp.maximum(i - 1, 0)

    return pl.pallas_call(
        _swa_prompt_kernel,
        grid=(b, N_BLK),
        in_specs=[
            pl.BlockSpec(memory_space=pltpu.SMEM),
            pl.BlockSpec(memory_space=pltpu.SMEM),
            pl.BlockSpec((1, BLK, D_MODEL), lambda bb, i: (bb, i, 0)),
            pl.BlockSpec((1, BLK, kvw), lambda bb, i: (bb, prev(bb, i), 4)),
            pl.BlockSpec((1, BLK, kvw), lambda bb, i: (bb, i, 4)),
            pl.BlockSpec((1, BLK, kvw), lambda bb, i: (bb, prev(bb, i), 5)),
            pl.BlockSpec((1, BLK, kvw), lambda bb, i: (bb, i, 5)),
        ],
        out_specs=pl.BlockSpec((1, BLK, D_MODEL), lambda bb, i: (bb, i, 0)),
        out_shape=jax.ShapeDtypeStruct((b, LP, D_MODEL), BF16),
        compiler_params=_params(("parallel", "parallel")),
        name="swa_prompt",
    )(slopes, sinks, qkv, qkv, qkv, qkv, qkv)


KPAD = 2 * WINDOW


def _swa_decode_kernel(slopes_ref, sinks_ref, q_ref, kb_ref, vb_ref, kn_ref, vn_ref,
                       o_ref, ko_ref, vo_ref, kf_ref, vf_ref):
    for buf, new, full, out in ((kb_ref, kn_ref, kf_ref, ko_ref), (vb_ref, vn_ref, vf_ref, vo_ref)):
        full[...] = jnp.zeros(full.shape, F32)
        full[0:WINDOW, :] = buf[0]
        full[WINDOW:WINDOW + DEC_SEQ, :] = new[0]
        out[0, 0:WINDOW - DEC_SEQ, :] = buf[0, DEC_SEQ:WINDOW, :]
        out[0, WINDOW - DEC_SEQ:WINDOW, :] = new[0]
    rows = B_GROUP * DEC_SEQ
    row = lax.broadcasted_iota(jnp.int32, (rows, KPAD), 0)
    col = lax.broadcasted_iota(jnp.int32, (rows, KPAD), 1)
    dist_i = (row & (DEC_SEQ - 1)) + WINDOW - col
    valid = (dist_i >= 0) & (dist_i <= WINDOW) & (col < WINDOW + DEC_SEQ)
    dist = dist_i.astype(F32)
    g_col = lax.broadcasted_iota(jnp.int32, (rows, 1), 0) >> 2
    for kv in range(B_KV):
        slope = jnp.zeros((rows, 1), F32)
        sink = jnp.zeros((rows, 1), F32)
        for g in range(B_GROUP):
            slope = jnp.where(g_col == g, slopes_ref[kv * B_GROUP + g], slope)
            sink = jnp.where(g_col == g, sinks_ref[kv * B_GROUP + g], sink)
        qh = (q_ref[0, kv] * (B_DH ** -0.5)).astype(BF16)
        kh = kf_ref[:, kv * B_DH:(kv + 1) * B_DH].astype(BF16)
        vh = vf_ref[:, kv * B_DH:(kv + 1) * B_DH].astype(BF16)
        o_ref[0, kv] = _swa_head(qh, kh, vh, dist, valid, slope, sink)


def swa_decode(q_r, k_buf, v_buf, k_new, v_new, sinks, slopes):
    bd = q_r.shape[0]
    kvw = B_KV * B_DH
    buf_spec = pl.BlockSpec((1, WINDOW, kvw), lambda b: (b, 0, 0))
    new_spec = pl.BlockSpec((1, DEC_SEQ, kvw), lambda b: (b, 0, 0))
    return pl.pallas_call(
        _swa_decode_kernel,
        grid=(bd,),
        in_specs=[
            pl.BlockSpec(memory_space=pltpu.SMEM),
            pl.BlockSpec(memory_space=pltpu.SMEM),
            pl.BlockSpec((1, B_KV, B_GROUP * DEC_SEQ, B_DH), lambda b: (b, 0, 0, 0)),
            buf_spec, buf_spec, new_spec, new_spec,
        ],
        out_specs=[pl.BlockSpec((1, B_KV, B_GROUP * DEC_SEQ, B_DH), lambda b: (b, 0, 0, 0)),
                   buf_spec, buf_spec],
        out_shape=[jax.ShapeDtypeStruct((bd, B_KV, B_GROUP * DEC_SEQ, B_DH), F32),
                   jax.ShapeDtypeStruct((bd, WINDOW, kvw), F32),
                   jax.ShapeDtypeStruct((bd, WINDOW, kvw), F32)],
        scratch_shapes=[pltpu.VMEM((KPAD, kvw), F32), pltpu.VMEM((KPAD, kvw), F32)],
        compiler_params=_params(("parallel",)),
        name="swa_decode",
    )(slopes, sinks, q_r, k_buf, v_buf, k_new, v_new)


def _silu(x):
    return x * (1.0 / (1.0 + jnp.exp(-x)))


def _ffn_begin(h_ref, o_ref):
    @pl.when(pl.program_id(1) == 0)
    def _():
        o_ref[...] = h_ref[...]


def _ffn_proj(xn, wg_ref, wu_ref):
    return (jnp.dot(xn, wg_ref[...], preferred_element_type=F32),
            jnp.dot(xn, wu_ref[...], preferred_element_type=F32))


def _ffn_gated(gate, up, gate_prev2, gate_prev1, cw, cb, wd_ref):
    gc = cb + (cw[0:1] * gate_prev2 + cw[1:2] * gate_prev1 + cw[2:3] * gate)
    act = (_silu(gc) * up).astype(BF16)
    return jnp.dot(act, wd_ref[...], preferred_element_type=F32)


def _ffn_prompt_kernel(xn_ref, h_ref, wg_ref, wu_ref, cw_ref, cb_ref, wd_ref, o_ref, st_ref,
                       tail_ref, *, tiles_per_seq, state_row, n_sub):
    i = pl.program_id(0)
    f = pl.program_id(1)
    _ffn_begin(h_ref, o_ref)

    @pl.when(i % tiles_per_seq == 0)
    def _():
        tail_ref[f] = jnp.zeros(tail_ref.shape[1:], F32)

    cw = cw_ref[...]
    cb = cb_ref[...]
    rs = xn_ref.shape[0] // n_sub
    row = lax.broadcasted_iota(jnp.int32, (rs, 1), 0)
    prev = tail_ref[f]
    cur = _ffn_proj(xn_ref[0:rs, :], wg_ref, wu_ref)
    for r in range(n_sub):
        nxt = _ffn_proj(xn_ref[(r + 1) * rs:(r + 2) * rs, :], wg_ref, wu_ref) if r + 1 < n_sub else None
        gate, up = cur
        p1 = prev[7:8]
        p2 = prev[6:7]
        g1 = jnp.where(row >= 1, pltpu.roll(gate, 1, 0), p1)
        g2 = jnp.where(row >= 2, pltpu.roll(gate, 2, 0), jnp.where(row == 1, p1, p2))
        o_ref[r * rs:(r + 1) * rs, :] += _ffn_gated(gate, up, g2, g1, cw, cb, wd_ref)
        prev = gate[rs - 8:rs]
        if r * rs <= state_row < (r + 1) * rs:
            st_ref[0] = gate[state_row - r * rs:state_row - r * rs + 8]
        cur = nxt
    tail_ref[f] = prev


def _ffn_sample_kernel(xn_ref, h_ref, wg_ref, wu_ref, cw_ref, cb_ref, wd_ref, p1_ref, p2_ref, o_ref, g_ref):
    _ffn_begin(h_ref, o_ref)
    t = lax.broadcasted_iota(jnp.int32, (xn_ref.shape[0], 1), 0) & (DEC_SEQ - 1)
    gate, up = _ffn_proj(xn_ref[...], wg_ref, wu_ref)
    g1 = jnp.where(t >= 1, pltpu.roll(gate, 1, 0), p1_ref[...])
    g2 = jnp.where(t >= 2, pltpu.roll(gate, 2, 0), p2_ref[...])
    g_ref[...] = gate
    o_ref[...] += _ffn_gated(gate, up, g2, g1, cw_ref[...], cb_ref[...], wd_ref)


def _ffn_common_specs(layer, tm, tf):
    nf = D_FF // tf
    return [
        pl.BlockSpec((tm, D_MODEL), lambda i, f: (i, 0), pipeline_mode=pl.Buffered(1)),
        pl.BlockSpec((tm, D_MODEL), lambda i, f: (i, 0), pipeline_mode=pl.Buffered(1)),
        pl.BlockSpec((None, D_MODEL, tf), lambda i, f: (layer, 0, f)),
        pl.BlockSpec((None, D_MODEL, tf), lambda i, f: (layer, 0, nf + f)),
        pl.BlockSpec((None, 3, tf), lambda i, f: (layer, 0, f)),
        pl.BlockSpec((None, 1, tf), lambda i, f: (layer, 0, f)),
        pl.BlockSpec((None, tf, D_MODEL), lambda i, f: (layer, f, 0)),
    ]


def ffn_prompt(xn, h, w_gu, conv_w, conv_b, w_down, layer, tm, tf):
    m = xn.shape[0]
    n_tiles = m // tm
    tiles_per_seq = LP // tm
    state_row = ((L_REAL - 2) % tm) // 8 * 8
    kern = functools.partial(_ffn_prompt_kernel, tiles_per_seq=tiles_per_seq, state_row=state_row,
                             n_sub=FFN_ROW_GROUPS)
    return pl.pallas_call(
        kern,
        grid=(n_tiles, D_FF // tf),
        in_specs=_ffn_common_specs(layer, tm, tf),
        out_specs=[pl.BlockSpec((tm, D_MODEL), lambda i, f: (i, 0)),
                   pl.BlockSpec((1, 8, tf), lambda i, f: (i, 0, f))],
        out_shape=[jax.ShapeDtypeStruct((m, D_MODEL), F32),
                   jax.ShapeDtypeStruct((n_tiles, 8, D_FF), F32)],
        scratch_shapes=[pltpu.VMEM((D_FF // tf, 8, tf), F32)],
        compiler_params=_params(("arbitrary", "arbitrary")),
        name="ffn_prompt",
    )(xn, h, w_gu, w_gu, conv_w, conv_b.reshape(DEPTH, 1, D_FF), w_down)


def ffn_sample(xn, h, w_gu, conv_w, conv_b, w_down, layer, p1, p2, tf):
    m = xn.shape[0]
    specs = _ffn_common_specs(layer, m, tf)
    specs += [pl.BlockSpec((m, tf), lambda i, f: (0, f)), pl.BlockSpec((m, tf), lambda i, f: (0, f))]
    return pl.pallas_call(
        _ffn_sample_kernel,
        grid=(1, D_FF // tf),
        in_specs=specs,
        out_specs=[pl.BlockSpec((m, D_MODEL), lambda i, f: (0, 0)),
                   pl.BlockSpec((m, tf), lambda i, f: (0, f))],
        out_shape=[jax.ShapeDtypeStruct((m, D_MODEL), F32),
                   jax.ShapeDtypeStruct((m, D_FF), F32)],
        compiler_params=_params(("arbitrary", "arbitrary")),
        name="ffn_sample",
    )(xn, h, w_gu, w_gu, conv_w, conv_b.reshape(DEPTH, 1, D_FF), w_down, p1, p2)


TM_PROMPT = LP // 2
TM_NORM = LP // 4
TN = 512
TF = 256
FFN_ROW_GROUPS = 4


def _alibi(n):
    return jnp.asarray(np.power(2.0, -8.0 * np.arange(1, n + 1, dtype=np.float32) / n).astype(np.float32))


def kernel(x_prompt, x_sample, cache_a_k, cache_a_v, state_b_k, state_b_v, state_conv, page_table, meta,
           norm_mix, norm_ffn, norm_final, w_qkv_a, lambda_a, subln_a, w_o_a, w_qkv_b, b_qkv_b, sinks_b,
           w_o_b, b_o_b, w_gu, conv_w, conv_b, w_down):
    bp, bd = BATCH, DEC_BATCH
    slopes_a = _alibi(A_HEADS)
    slopes_b = _alibi(B_HEADS)
    w_qkv_a, w_o_a, w_qkv_b, w_o_b, w_gu, w_down = (
        w.astype(BF16) for w in (w_qkv_a, w_o_a, w_qkv_b, w_o_b, w_gu, w_down))
    n_pool = cache_a_k.shape[1]
    cache_k = cache_a_k.reshape(cache_a_k.shape[0], n_pool, PAGE * A_KV, 2 * A_DH)
    cache_v = cache_a_v.reshape(cache_a_v.shape[0], n_pool, PAGE * A_KV, 2 * A_DH)

    hp = jnp.concatenate([jnp.broadcast_to(meta[None], (bp, N_META, D_MODEL)), x_prompt,
                          jnp.zeros((bp, LP - L_REAL, D_MODEL), F32)], axis=1).reshape(M_PROMPT, D_MODEL)
    hs = x_sample.reshape(M_SAMPLE, D_MODEL)
    ka_p, va_p, ka_s, va_s = [], [], [], []
    kb_p, vb_p, kb_s, vb_s = [], [], [], []
    cv_p, cv_s = [], []
    for i in range(DEPTH):
        j = i // 2
        xnp = rmsnorm(hp, norm_mix[i], TM_NORM, BF16)
        xns = rmsnorm(hs, norm_mix[i], M_SAMPLE, BF16)
        if i % 2 == 0:
            lam_init = 0.8 - 0.6 * math.exp(-0.3 * i)
            n_qkv = w_qkv_a.shape[2]
            qkv_p = matmul(xnp, w_qkv_a, j, n_qkv, TM_PROMPT, TN).reshape(bp, LP, n_qkv)
            o_p = diff_prompt(qkv_p, lambda_a[j], subln_a[j], slopes_a, lam_init)
            hp = matmul(o_p.reshape(M_PROMPT, D_MODEL), w_o_a, j, D_MODEL, TM_PROMPT, TN, res=hp)
            ka_p.append(qkv_p[:, :L_REAL, 2048:3072].reshape(bp, L_REAL, A_KV, 2 * A_DH))
            va_p.append(qkv_p[:, :L_REAL, 3072:4096].reshape(bp, L_REAL, A_KV, 2 * A_DH))

            qkv_s = matmul(xns, w_qkv_a, j, n_qkv, M_SAMPLE, TN).reshape(bd, DEC_SEQ, n_qkv)
            q_r = qkv_s[:, :, :2048].reshape(bd, DEC_SEQ, A_KV, 2, 2 * A_DH)
            q_r = q_r.transpose(0, 2, 3, 1, 4).reshape(bd, A_KV, 2 * DEC_SEQ, 2 * A_DH)
            k_new = qkv_s[:, :, 2048:3072]
            v_new = qkv_s[:, :, 3072:4096]
            o_s = diff_decode(q_r, k_new, v_new, cache_k, cache_v, page_table, j,
                              lambda_a[j], subln_a[j], slopes_a, lam_init)
            o_s = o_s.reshape(bd, A_KV, 2, DEC_SEQ, 2 * A_DH).transpose(0, 3, 1, 2, 4).reshape(M_SAMPLE, D_MODEL)
            hs = matmul(o_s, w_o_a, j, D_MODEL, M_SAMPLE, TN, res=hs)
            ka_s.append(k_new.reshape(bd, DEC_SEQ, A_KV, 2 * A_DH))
            va_s.append(v_new.reshape(bd, DEC_SEQ, A_KV, 2 * A_DH))
        else:
            n_qkv = w_qkv_b.shape[2]
            kvw = B_KV * B_DH
            qkv_p = matmul(xnp, w_qkv_b, j, n_qkv, TM_PROMPT, TN, bias=b_qkv_b[j]).reshape(bp, LP, n_qkv)
            o_p = swa_prompt(qkv_p, sinks_b[j], slopes_b)
            hp = matmul(o_p.reshape(M_PROMPT, D_MODEL), w_o_b, j, D_MODEL, TM_PROMPT, TN, bias=b_o_b[j], res=hp)
            kb_p.append(qkv_p[:, L_REAL - WINDOW:L_REAL, 2048:2048 + kvw].reshape(bp, WINDOW, B_KV, B_DH))
            vb_p.append(qkv_p[:, L_REAL - WINDOW:L_REAL, 2048 + kvw:].reshape(bp, WINDOW, B_KV, B_DH))

            qkv_s = matmul(xns, w_qkv_b, j, n_qkv, M_SAMPLE, TN, bias=b_qkv_b[j]).reshape(bd, DEC_SEQ, n_qkv)
            q_r = qkv_s[:, :, :2048].reshape(bd, DEC_SEQ, B_KV, B_GROUP, B_DH)
            q_r = q_r.transpose(0, 2, 3, 1, 4).reshape(bd, B_KV, B_GROUP * DEC_SEQ, B_DH)
            o_s, k_st, v_st = swa_decode(q_r, state_b_k[j].reshape(bd, WINDOW, kvw),
                                         state_b_v[j].reshape(bd, WINDOW, kvw),
                                         qkv_s[:, :, 2048:2048 + kvw], qkv_s[:, :, 2048 + kvw:],
                                         sinks_b[j], slopes_b)
            o_s = o_s.reshape(bd, B_KV, B_GROUP, DEC_SEQ, B_DH).transpose(0, 3, 1, 2, 4).reshape(M_SAMPLE, D_MODEL)
            hs = matmul(o_s, w_o_b, j, D_MODEL, M_SAMPLE, TN, bias=b_o_b[j], res=hs)
            kb_s.append(k_st.reshape(bd, WINDOW, B_KV, B_DH))
            vb_s.append(v_st.reshape(bd, WINDOW, B_KV, B_DH))

        xnp = rmsnorm(hp, norm_ffn[i], TM_NORM, BF16)
        hp, tails = ffn_prompt(xnp, hp, w_gu, conv_w, conv_b, w_down, i, TM_PROMPT, TF)
        off = (L_REAL - 2) % TM_PROMPT % 8
        cv_p.append(tails[LP // TM_PROMPT - 1::LP // TM_PROMPT, off:off + 2])

        xns = rmsnorm(hs, norm_ffn[i], M_SAMPLE, BF16)
        st = state_conv[i]
        zero = jnp.zeros((bd, 1, D_FF), F32)
        p1 = jnp.concatenate([st[:, 1:2], zero, zero, zero], axis=1).reshape(M_SAMPLE, D_FF)
        p2 = jnp.concatenate([st[:, 0:1], st[:, 1:2], zero, zero], axis=1).reshape(M_SAMPLE, D_FF)
        hs, gate_s = ffn_sample(xns, hs, w_gu, conv_w, conv_b, w_down, i, p1, p2, TF)
        cv_s.append(gate_s.reshape(bd, DEC_SEQ, D_FF)[:, DEC_SEQ - 2:])

    y_prompt = rmsnorm(hp, norm_final, TM_NORM, F32).reshape(bp, LP, D_MODEL)[:, N_META:L_REAL]
    y_sample = rmsnorm(hs, norm_final, M_SAMPLE, F32).reshape(bd, DEC_SEQ, D_MODEL)
    return (y_prompt, y_sample, jnp.stack(ka_p), jnp.stack(va_p), jnp.stack(ka_s), jnp.stack(va_s),
            jnp.stack(kb_p), jnp.stack(vb_p), jnp.stack(kb_s), jnp.stack(vb_s),
            jnp.stack(cv_p), jnp.stack(cv_s))
```

```python
import functools
import math

import numpy as np
import jax
import jax.numpy as jnp
from jax import lax
from jax.experimental import pallas as pl
from jax.experimental.pallas import tpu as pltpu

F32 = jnp.float32
BF16 = jnp.bfloat16

D_MODEL = 2048
BATCH = 4
SEQ = 2048
DEPTH = 4
DEC_BATCH = 32
DEC_SEQ = 4
PAGE = 128
N_META = 16
EPS = 1e-6
NEG_INF = -1e30
LOG2E = math.log2(math.e)
A_HEADS = 16
A_KV = 8
A_DH = 64
B_HEADS = 32
B_KV = 8
B_GROUP = 4
B_DH = 64
WINDOW = 128
D_FF = 5632

L_REAL = N_META + SEQ
BLK = 128
LP = -(-L_REAL // BLK) * BLK
N_BLK = LP // BLK
M_PROMPT = BATCH * LP
M_SAMPLE = DEC_BATCH * DEC_SEQ

VMEM_LIMIT = 56 * 1024 * 1024


def _params(sem, vmem=VMEM_LIMIT):
    return pltpu.CompilerParams(dimension_semantics=sem, vmem_limit_bytes=vmem)


def _rmsnorm_kernel(x_ref, g_ref, o_ref):
    x = x_ref[...]
    ms = jnp.mean(x * x, axis=-1, keepdims=True)
    o_ref[...] = (x * lax.rsqrt(ms + EPS) * g_ref[...]).astype(o_ref.dtype)


def rmsnorm(x, g, tm, out_dtype):
    m, d = x.shape
    return pl.pallas_call(
        _rmsnorm_kernel,
        grid=(m // tm,),
        in_specs=[pl.BlockSpec((tm, d), lambda i: (i, 0)),
                  pl.BlockSpec((1, d), lambda i: (0, 0))],
        out_specs=pl.BlockSpec((tm, d), lambda i: (i, 0)),
        out_shape=jax.ShapeDtypeStruct((m, d), out_dtype),
        compiler_params=_params(("parallel",)),
        name="rmsnorm",
    )(x, g.reshape(1, d))


def _matmul_kernel(*refs, has_bias, has_res):
    x_ref, w_ref = refs[0], refs[1]
    pos = 2
    b_ref = r_ref = None
    if has_bias:
        b_ref = refs[pos]
        pos += 1
    if has_res:
        r_ref = refs[pos]
        pos += 1
    o_ref = refs[pos]
    acc = jnp.dot(x_ref[...].astype(BF16), w_ref[...], preferred_element_type=F32)
    if has_bias:
        acc = acc + b_ref[...]
    if has_res:
        acc = acc + r_ref[...]
    o_ref[...] = acc


def matmul(x, w, layer, n_out, tm, tn, bias=None, res=None):
    m, k = x.shape
    in_specs = [pl.BlockSpec((tm, k), lambda j, i: (i, 0)),
                pl.BlockSpec((None, k, tn), lambda j, i: (layer, 0, j))]
    args = [x, w]
    if bias is not None:
        in_specs.append(pl.BlockSpec((1, tn), lambda j, i: (0, j)))
        args.append(bias.reshape(1, n_out))
    if res is not None:
        in_specs.append(pl.BlockSpec((tm, tn), lambda j, i: (i, j)))
        args.append(res)
    return pl.pallas_call(
        functools.partial(_matmul_kernel, has_bias=bias is not None, has_res=res is not None),
        grid=(n_out // tn, m // tm),
        in_specs=in_specs,
        out_specs=pl.BlockSpec((tm, tn), lambda j, i: (i, j)),
        out_shape=jax.ShapeDtypeStruct((m, n_out), F32),
        compiler_params=_params(("arbitrary", "arbitrary")),
        name="matmul",
    )(*args)


def _diff_lambda(lam_ref, lam_init):
    lp = lam_ref[...]
    a = jnp.sum(lp[0:1] * lp[1:2], axis=-1, keepdims=True)
    b = jnp.sum(lp[2:3] * lp[3:4], axis=-1, keepdims=True)
    return jnp.exp(a) - jnp.exp(b) + lam_init


def _subln(o, sub_ref, lam_init):
    ms = jnp.mean(o * o, axis=-1, keepdims=True)
    return (o * lax.rsqrt(ms + EPS) * sub_ref[...]) * (1.0 - lam_init)


A_HPS = 2
A_CHUNKS = 2 * A_HPS


def _diff_prompt_kernel(slopes_ref, q_ref, k_ref, v_ref, lam_ref, sub_ref, o_ref,
                        kb_ref, vt_ref, qt_ref, m_ref, l_ref, acc_ref, *, lam_init):
    hp = pl.program_id(1)
    i = pl.program_id(2)

    @pl.when(i == 0)
    def _():
        for hh in range(A_HPS):
            for jj in range(N_BLK):
                kb_ref[hh * N_BLK + jj] = k_ref[0, jj * BLK:(jj + 1) * BLK, hh * BLK:(hh + 1) * BLK].astype(BF16)
                vt_ref[hh * N_BLK + jj] = v_ref[0, jj * BLK:(jj + 1) * BLK, hh * BLK:(hh + 1) * BLK].T.astype(BF16)

    d_io = lax.broadcasted_iota(jnp.int32, (BLK, BLK), 0)
    for ch in range(A_CHUNKS):
        qgt = (q_ref[0, :, ch * BLK:(ch + 1) * BLK] * (A_DH ** -0.5 * LOG2E)).T
        qt_ref[:, (2 * ch) * BLK:(2 * ch + 1) * BLK] = jnp.where(d_io < A_DH, qgt, 0.0).astype(BF16)
        qt_ref[:, (2 * ch + 1) * BLK:(2 * ch + 2) * BLK] = jnp.where(d_io >= A_DH, qgt, 0.0).astype(BF16)
    m_ref[...] = jnp.full(m_ref.shape, NEG_INF, F32)
    l_ref[...] = jnp.zeros(l_ref.shape, F32)
    acc_ref[...] = jnp.zeros(acc_ref.shape, F32)

    k_io = lax.broadcasted_iota(jnp.int32, (BLK, 2 * BLK), 0)
    c_io = lax.broadcasted_iota(jnp.int32, (BLK, 2 * BLK), 1)
    rel = ((c_io & (BLK - 1)) - k_io).astype(F32)
    slope = [slopes_ref[A_CHUNKS * hp + ch] * LOG2E for ch in range(A_CHUNKS)]
    srel = [slope[ch] * rel for ch in range(A_CHUNKS)]
    cols = [slice(2 * ch * BLK, (2 * ch + 2) * BLK) for ch in range(A_CHUNKS)]

    def scores(j, masked):
        ts = []
        for ch in range(A_CHUNKS):
            kj = kb_ref[(ch // 2) * N_BLK + j]
            t = jnp.dot(kj, qt_ref[:, cols[ch]], preferred_element_type=F32) - srel[ch]
            ts.append(jnp.where(rel >= 0, t, NEG_INF) if masked else t)
        return ts

    def softmax(j, ts):
        off = ((i - j) * BLK).astype(F32)
        ps = []
        alphas = []
        for ch in range(A_CHUNKS):
            shift = slope[ch] * off
            m_old = m_ref[:, cols[ch]]
            m_new = jnp.maximum(m_old, jnp.max(ts[ch], axis=0, keepdims=True) - shift)
            alpha = jnp.exp2(m_old - m_new)
            p = jnp.exp2(ts[ch] - (m_new + shift))
            l_ref[:, cols[ch]] = alpha * l_ref[:, cols[ch]] + jnp.sum(p, axis=0, keepdims=True)
            m_ref[:, cols[ch]] = m_new
            ps.append(p.astype(BF16))
            alphas.append(alpha)
        return ps, alphas

    def values(j, ps, alphas):
        for ch in range(A_CHUNKS):
            vtj = vt_ref[(ch // 2) * N_BLK + j]
            acc_ref[:, cols[ch]] = (alphas[ch] * acc_ref[:, cols[ch]]
                                    + jnp.dot(vtj, ps[ch], preferred_element_type=F32))

    def group(start, count, mask_last):
        ts = scores(start, mask_last and count == 1)
        for u in range(count):
            nxt = scores(start + u + 1, mask_last and u + 2 == count) if u + 1 < count else None
            ps, alphas = softmax(start + u, ts)
            values(start + u, ps, alphas)
            ts = nxt

    n = i + 1
    nq = n // 4
    r = n - 4 * nq
    for rr in (1, 2, 3):
        @pl.when(jnp.logical_and(r == rr, nq == 0))
        def _(rr=rr):
            group(0, rr, True)

        @pl.when(jnp.logical_and(r == rr, nq > 0))
        def _(rr=rr):
            group(0, rr, False)

    def body(q, carry):
        group(r + 4 * q, 4, False)
        return carry

    lax.fori_loop(0, nq - 1, body, 0)

    @pl.when(nq > 0)
    def _():
        group(r + 4 * (nq - 1), 4, True)

    lam = _diff_lambda(lam_ref, lam_init)
    ot = acc_ref[...] / l_ref[...]
    for ch in range(A_CHUNKS):
        ogt = ot[:, (2 * ch) * BLK:(2 * ch + 1) * BLK] - lam * ot[:, (2 * ch + 1) * BLK:(2 * ch + 2) * BLK]
        o_ref[0, :, ch * BLK:(ch + 1) * BLK] = _subln(ogt.T, sub_ref, lam_init).astype(o_ref.dtype)


def diff_prompt(qkv, lam_p, subln, slopes, lam_init):
    b = qkv.shape[0]
    kern = functools.partial(_diff_prompt_kernel, lam_init=lam_init)
    qw = A_CHUNKS * BLK
    kw = A_HPS * BLK
    k_blk = 2 * A_KV * BLK // kw
    v_blk = k_blk + A_KV * BLK // kw
    return pl.pallas_call(
        kern,
        grid=(b, A_KV // A_HPS, N_BLK),
        in_specs=[
            pl.BlockSpec(memory_space=pltpu.SMEM),
            pl.BlockSpec((1, BLK, qw), lambda bb, h, i: (bb, i, h)),
            pl.BlockSpec((1, LP, kw), lambda bb, h, i: (bb, 0, k_blk + h)),
            pl.BlockSpec((1, LP, kw), lambda bb, h, i: (bb, 0, v_blk + h)),
            pl.BlockSpec((4, A_DH), lambda bb, h, i: (0, 0)),
            pl.BlockSpec((1, BLK), lambda bb, h, i: (0, 0)),
        ],
        out_specs=pl.BlockSpec((1, BLK, qw), lambda bb, h, i: (bb, i, h)),
        out_shape=jax.ShapeDtypeStruct((b, LP, 2 * A_KV * BLK), BF16),
        scratch_shapes=[pltpu.VMEM((A_HPS * N_BLK, BLK, BLK), BF16),
                        pltpu.VMEM((A_HPS * N_BLK, BLK, BLK), BF16),
                        pltpu.VMEM((BLK, 2 * qw), BF16),
                        pltpu.VMEM((1, 2 * qw), F32),
                        pltpu.VMEM((1, 2 * qw), F32),
                        pltpu.VMEM((BLK, 2 * qw), F32)],
        compiler_params=_params(("arbitrary", "arbitrary", "arbitrary")),
        name="diff_prompt",
    )(slopes, qkv, qkv, qkv, lam_p, subln.reshape(1, BLK))


PAGES_PER_STEP = 8
DEC_ROWS = A_KV * 16


def _diff_decode_kernel(*refs, lam_init, past):
    pt_ref, q_ref, kn_ref, vn_ref, slope_ref, tq_ref = refs[:6]
    kp = refs[6:6 + PAGES_PER_STEP]
    vp = refs[6 + PAGES_PER_STEP:6 + 2 * PAGES_PER_STEP]
    lam_ref, sub_ref, o_ref, qs_ref, m_ref, l_ref, acc_ref, knp_ref, vnp_ref = refs[6 + 2 * PAGES_PER_STEP:]
    del pt_ref
    c = pl.program_id(1)
    n_c = pl.num_programs(1)
    chunk = PAGES_PER_STEP * PAGE
    nt = (((1,), (1,)), ((), ()))

    @pl.when(c == 0)
    def _():
        lane8 = lax.broadcasted_iota(jnp.int32, (8, BLK), 1)
        for h in range(A_KV):
            q8 = q_ref[0, h] * (A_DH ** -0.5)
            qs_ref[h, 0:8, :] = jnp.where(lane8 < A_DH, q8, 0.0).astype(BF16)
            qs_ref[h, 8:16, :] = jnp.where(lane8 >= A_DH, q8, 0.0).astype(BF16)
        m_ref[...] = jnp.full(m_ref.shape, NEG_INF, F32)
        l_ref[...] = jnp.zeros(l_ref.shape, F32)
        acc_ref[...] = jnp.zeros(acc_ref.shape, F32)

    def update(s_tiles, v_of):
        mx = s_tiles[0]
        for s in s_tiles[1:]:
            mx = jnp.maximum(mx, s)
        m_old = m_ref[...]
        m_new = jnp.maximum(m_old, jnp.max(mx, axis=-1, keepdims=True))
        alpha = jnp.exp(m_old - m_new)
        m_b = jnp.broadcast_to(m_new, (DEC_ROWS, BLK))
        ps = [jnp.exp(s - m_b) for s in s_tiles]
        psum = ps[0]
        for p in ps[1:]:
            psum = psum + p
        l_ref[...] = alpha * l_ref[...] + jnp.sum(psum, axis=-1, keepdims=True)
        pv_rows = []
        for h in range(A_KV):
            pv = None
            for n, p in enumerate(ps):
                part = jnp.dot(p[h * 16:(h + 1) * 16].astype(BF16), v_of(n, h), preferred_element_type=F32)
                pv = part if pv is None else pv + part
            pv_rows.append(pv)
        acc_ref[...] = alpha * acc_ref[...] + jnp.concatenate(pv_rows, axis=0)
        m_ref[...] = m_new

    slope = slope_ref[...]
    tq = tq_ref[...]
    kcol = lax.broadcasted_iota(jnp.int32, (1, PAGE), 1)
    s_tiles = []
    for n in range(PAGES_PER_STEP):
        rows = []
        for h in range(A_KV):
            kh = kp[n][pl.ds(h, PAGE, stride=A_KV), :].astype(BF16)
            rows.append(lax.dot_general(qs_ref[h], kh, nt, preferred_element_type=F32))
        k_pos = (c * chunk + n * PAGE + kcol).astype(F32)
        s_tiles.append(jnp.concatenate(rows, axis=0) - slope * (tq - k_pos))
    update(s_tiles, lambda n, h: vp[n][pl.ds(h, PAGE, stride=A_KV), :].astype(BF16))

    @pl.when(c == n_c - 1)
    def _():
        knp_ref[...] = jnp.zeros(knp_ref.shape, F32)
        vnp_ref[...] = jnp.zeros(vnp_ref.shape, F32)
        knp_ref[0:DEC_SEQ, :] = kn_ref[0]
        vnp_ref[0:DEC_SEQ, :] = vn_ref[0]
        rows = []
        for h in range(A_KV):
            kh = knp_ref[:, h * BLK:(h + 1) * BLK].astype(BF16)
            rows.append(lax.dot_general(qs_ref[h], kh, nt, preferred_element_type=F32))
        distn = tq - (float(past) + kcol.astype(F32))
        s_new = jnp.where(distn >= 0, jnp.concatenate(rows, axis=0) - slope * distn, NEG_INF)
        update([s_new], lambda n, h: vnp_ref[:, h * BLK:(h + 1) * BLK].astype(BF16))
        lam = _diff_lambda(lam_ref, lam_init)
        o = acc_ref[...] / l_ref[...]
        for h in range(A_KV):
            og = o[h * 16:h * 16 + 8] - lam * o[h * 16 + 8:h * 16 + 16]
            o_ref[0, h] = _subln(og, sub_ref, lam_init)


def diff_decode(q_r, k_new, v_new, cache_k, cache_v, page_table, layer, lam_p, subln, slopes, lam_init):
    bd = q_r.shape[0]
    n_pages = page_table.shape[1]
    past = n_pages * PAGE
    n_c = n_pages // PAGES_PER_STEP
    r = np.arange(DEC_ROWS)
    slope_rows = jnp.broadcast_to(slopes[2 * (r // 16) + (r // DEC_SEQ) % 2][:, None], (DEC_ROWS, BLK))
    tq_rows = jnp.asarray(np.broadcast_to((past + r % DEC_SEQ).astype(np.float32)[:, None], (DEC_ROWS, BLK)))

    def page_spec(n):
        return pl.BlockSpec((None, None, PAGE * A_KV, BLK),
                            lambda b, c, pt: (layer, pt[b, c * PAGES_PER_STEP + n], 0, 0))

    const_spec = pl.BlockSpec((DEC_ROWS, BLK), lambda b, c, pt: (0, 0))
    in_specs = [
        pl.BlockSpec((1, A_KV, 8, BLK), lambda b, c, pt: (b, 0, 0, 0)),
        pl.BlockSpec((1, DEC_SEQ, A_KV * BLK), lambda b, c, pt: (b, 0, 0)),
        pl.BlockSpec((1, DEC_SEQ, A_KV * BLK), lambda b, c, pt: (b, 0, 0)),
        const_spec, const_spec,
    ]
    in_specs += [page_spec(n) for n in range(PAGES_PER_STEP)]
    in_specs += [page_spec(n) for n in range(PAGES_PER_STEP)]
    in_specs += [pl.BlockSpec((4, A_DH), lambda b, c, pt: (0, 0)),
                 pl.BlockSpec((1, BLK), lambda b, c, pt: (0, 0))]
    kern = functools.partial(_diff_decode_kernel, lam_init=lam_init, past=past)
    return pl.pallas_call(
        kern,
        grid_spec=pltpu.PrefetchScalarGridSpec(
            num_scalar_prefetch=1,
            grid=(bd, n_c),
            in_specs=in_specs,
            out_specs=pl.BlockSpec((1, A_KV, 8, BLK), lambda b, c, pt: (b, 0, 0, 0)),
            scratch_shapes=[pltpu.VMEM((A_KV, 16, BLK), BF16),
                            pltpu.VMEM((DEC_ROWS, 1), F32),
                            pltpu.VMEM((DEC_ROWS, 1), F32),
                            pltpu.VMEM((DEC_ROWS, BLK), F32),
                            pltpu.VMEM((PAGE, A_KV * BLK), F32),
                            pltpu.VMEM((PAGE, A_KV * BLK), F32)],
        ),
        out_shape=jax.ShapeDtypeStruct((bd, A_KV, 8, BLK), F32),
        compiler_params=_params(("arbitrary", "arbitrary")),
        name="diff_decode",
    )(page_table, q_r, k_new, v_new, slope_rows, tq_rows,
      *([cache_k] * PAGES_PER_STEP), *([cache_v] * PAGES_PER_STEP), lam_p, subln.reshape(1, BLK))


def _swa_head(qh, kh, vh, dist, valid, slope, sink):
    s = lax.dot_general(qh, kh, (((1,), (1,)), ((), ())), preferred_element_type=F32)
    s = jnp.where(valid, s - slope * dist, NEG_INF)
    m = jnp.maximum(jnp.max(s, axis=-1, keepdims=True), sink)
    p = jnp.exp(s - m)
    p = p / (jnp.sum(p, axis=-1, keepdims=True) + jnp.exp(sink - m))
    return jnp.dot(p.astype(BF16), vh, preferred_element_type=F32)


def _swa_prompt_kernel(slopes_ref, sinks_ref, q_ref, kp_ref, kc_ref, vp_ref, vc_ref, o_ref):
    i = pl.program_id(1)
    kk = jnp.concatenate([kp_ref[0], kc_ref[0]], axis=0).astype(BF16)
    vv = jnp.concatenate([vp_ref[0], vc_ref[0]], axis=0).astype(BF16)
    row = lax.broadcasted_iota(jnp.int32, (BLK, 2 * BLK), 0)
    col = lax.broadcasted_iota(jnp.int32, (BLK, 2 * BLK), 1)
    dist_i = row + BLK - col
    k_pos = (i - 1) * BLK + col
    valid = (dist_i >= 0) & (dist_i <= WINDOW) & (k_pos >= 0)
    dist = dist_i.astype(F32)
    nt = (((1,), (1,)), ((), ()))
    scores = []
    for kv in range(B_KV):
        qs = jnp.concatenate([q_ref[0, :, hh * B_DH:(hh + 1) * B_DH]
                              for hh in range(kv * B_GROUP, (kv + 1) * B_GROUP)], axis=0)
        qs = (qs * (B_DH ** -0.5)).astype(BF16)
        scores.append(lax.dot_general(qs, kk[:, kv * B_DH:(kv + 1) * B_DH], nt, preferred_element_type=F32))
    probs = []
    for kv in range(B_KV):
        ps = []
        for g in range(B_GROUP):
            hh = kv * B_GROUP + g
            s = scores[kv][g * BLK:(g + 1) * BLK]
            s = jnp.where(valid, s - slopes_ref[hh] * dist, NEG_INF)
            sink = sinks_ref[hh]
            m = jnp.maximum(jnp.max(s, axis=-1, keepdims=True), sink)
            p = jnp.exp(s - m)
            p = p / (jnp.sum(p, axis=-1, keepdims=True) + jnp.exp(sink - m))
            ps.append(p.astype(BF16))
        probs.append(jnp.concatenate(ps, axis=0))
    outs = [jnp.dot(probs[kv], vv[:, kv * B_DH:(kv + 1) * B_DH], preferred_element_type=F32)
            for kv in range(B_KV)]
    for pair in range(B_HEADS // 2):
        kv, g = (2 * pair) // B_GROUP, (2 * pair) % B_GROUP
        both = jnp.concatenate([outs[kv][g * BLK:(g + 1) * BLK], outs[kv][(g + 1) * BLK:(g + 2) * BLK]], axis=-1)
        o_ref[0, :, pair * BLK:(pair + 1) * BLK] = both.astype(o_ref.dtype)


def swa_prompt(qkv, sinks, slopes):
    b = qkv.shape[0]
    kvw = B_KV * B_DH

    def prev(bb, i):
        return jnp.maximum(i - 1, 0)

    return pl.pallas_call(
        _swa_prompt_kernel,
        grid=(b, N_BLK),
        in_specs=[
            pl.BlockSpec(memory_space=pltpu.SMEM),
            pl.BlockSpec(memory_space=pltpu.SMEM),
            pl.BlockSpec((1, BLK, D_MODEL), lambda bb, i: (bb, i, 0)),
            pl.BlockSpec((1, BLK, kvw), lambda bb, i: (bb, prev(bb, i), 4)),
            pl.BlockSpec((1, BLK, kvw), lambda bb, i: (bb, i, 4)),
            pl.BlockSpec((1, BLK, kvw), lambda bb, i: (bb, prev(bb, i), 5)),
            pl.BlockSpec((1, BLK, kvw), lambda bb, i: (bb, i, 5)),
        ],
        out_specs=pl.BlockSpec((1, BLK, D_MODEL), lambda bb, i: (bb, i, 0)),
        out_shape=jax.ShapeDtypeStruct((b, LP, D_MODEL), BF16),
        compiler_params=_params(("parallel", "parallel")),
        name="swa_prompt",
    )(slopes, sinks, qkv, qkv, qkv, qkv, qkv)


KPAD = 2 * WINDOW


def _swa_decode_kernel(slopes_ref, sinks_ref, q_ref, kb_ref, vb_ref, kn_ref, vn_ref,
                       o_ref, ko_ref, vo_ref, kf_ref, vf_ref):
    for buf, new, full, out in ((kb_ref, kn_ref, kf_ref, ko_ref), (vb_ref, vn_ref, vf_ref, vo_ref)):
        full[...] = jnp.zeros(full.shape, F32)
        full[0:WINDOW, :] = buf[0]
        full[WINDOW:WINDOW + DEC_SEQ, :] = new[0]
        out[0, 0:WINDOW - DEC_SEQ, :] = buf[0, DEC_SEQ:WINDOW, :]
        out[0, WINDOW - DEC_SEQ:WINDOW, :] = new[0]
    rows = B_GROUP * DEC_SEQ
    row = lax.broadcasted_iota(jnp.int32, (rows, KPAD), 0)
    col = lax.broadcasted_iota(jnp.int32, (rows, KPAD), 1)
    dist_i = (row & (DEC_SEQ - 1)) + WINDOW - col
    valid = (dist_i >= 0) & (dist_i <= WINDOW) & (col < WINDOW + DEC_SEQ)
    dist = dist_i.astype(F32)
    g_col = lax.broadcasted_iota(jnp.int32, (rows, 1), 0) >> 2
    for kv in range(B_KV):
        slope = jnp.zeros((rows, 1), F32)
        sink = jnp.zeros((rows, 1), F32)
        for g in range(B_GROUP):
            slope = jnp.where(g_col == g, slopes_ref[kv * B_GROUP + g], slope)
            sink = jnp.where(g_col == g, sinks_ref[kv * B_GROUP + g], sink)
        qh = (q_ref[0, kv] * (B_DH ** -0.5)).astype(BF16)
        kh = kf_ref[:, kv * B_DH:(kv + 1) * B_DH].astype(BF16)
        vh = vf_ref[:, kv * B_DH:(kv + 1) * B_DH].astype(BF16)
        o_ref[0, kv] = _swa_head(qh, kh, vh, dist, valid, slope, sink)


def swa_decode(q_r, k_buf, v_buf, k_new, v_new, sinks, slopes):
    bd = q_r.shape[0]
    kvw = B_KV * B_DH
    buf_spec = pl.BlockSpec((1, WINDOW, kvw), lambda b: (b, 0, 0))
    new_spec = pl.BlockSpec((1, DEC_SEQ, kvw), lambda b: (b, 0, 0))
    return pl.pallas_call(
        _swa_decode_kernel,
        grid=(bd,),
        in_specs=[
            pl.BlockSpec(memory_space=pltpu.SMEM),
            pl.BlockSpec(memory_space=pltpu.SMEM),
            pl.BlockSpec((1, B_KV, B_GROUP * DEC_SEQ, B_DH), lambda b: (b, 0, 0, 0)),
            buf_spec, buf_spec, new_spec, new_spec,
        ],
        out_specs=[pl.BlockSpec((1, B_KV, B_GROUP * DEC_SEQ, B_DH), lambda b: (b, 0, 0, 0)),
                   buf_spec, buf_spec],
        out_shape=[jax.ShapeDtypeStruct((bd, B_KV, B_GROUP * DEC_SEQ, B_DH), F32),
                   jax.ShapeDtypeStruct((bd, WINDOW, kvw), F32),
                   jax.ShapeDtypeStruct((bd, WINDOW, kvw), F32)],
        scratch_shapes=[pltpu.VMEM((KPAD, kvw), F32), pltpu.VMEM((KPAD, kvw), F32)],
        compiler_params=_params(("parallel",)),
        name="swa_decode",
    )(slopes, sinks, q_r, k_buf, v_buf, k_new, v_new)


def _silu(x):
    return x * (1.0 / (1.0 + jnp.exp(-x)))


def _ffn_begin(h_ref, o_ref):
    @pl.when(pl.program_id(1) == 0)
    def _():
        o_ref[...] = h_ref[...]


def _ffn_proj(xn, wg_ref, wu_ref):
    return (jnp.dot(xn, wg_ref[...], preferred_element_type=F32),
            jnp.dot(xn, wu_ref[...], preferred_element_type=F32))


def _ffn_gated(gate, up, gate_prev2, gate_prev1, cw, cb, wd_ref):
    gc = cb + (cw[0:1] * gate_prev2 + cw[1:2] * gate_prev1 + cw[2:3] * gate)
    act = (_silu(gc) * up).astype(BF16)
    return jnp.dot(act, wd_ref[...], preferred_element_type=F32)


def _ffn_prompt_kernel(xn_ref, h_ref, wg_ref, wu_ref, cw_ref, cb_ref, wd_ref, o_ref, st_ref,
                       tail_ref, *, tiles_per_seq, state_row, n_sub):
    i = pl.program_id(0)
    f = pl.program_id(1)
    _ffn_begin(h_ref, o_ref)

    @pl.when(i % tiles_per_seq == 0)
    def _():
        tail_ref[f] = jnp.zeros(tail_ref.shape[1:], F32)

    cw = cw_ref[...]
    cb = cb_ref[...]
    rs = xn_ref.shape[0] // n_sub
    row = lax.broadcasted_iota(jnp.int32, (rs, 1), 0)
    prev = tail_ref[f]
    cur = _ffn_proj(xn_ref[0:rs, :], wg_ref, wu_ref)
    for r in range(n_sub):
        nxt = _ffn_proj(xn_ref[(r + 1) * rs:(r + 2) * rs, :], wg_ref, wu_ref) if r + 1 < n_sub else None
        gate, up = cur
        p1 = prev[7:8]
        p2 = prev[6:7]
        g1 = jnp.where(row >= 1, pltpu.roll(gate, 1, 0), p1)
        g2 = jnp.where(row >= 2, pltpu.roll(gate, 2, 0), jnp.where(row == 1, p1, p2))
        o_ref[r * rs:(r + 1) * rs, :] += _ffn_gated(gate, up, g2, g1, cw, cb, wd_ref)
        prev = gate[rs - 8:rs]
        if r * rs <= state_row < (r + 1) * rs:
            st_ref[0] = gate[state_row - r * rs:state_row - r * rs + 8]
        cur = nxt
    tail_ref[f] = prev


def _ffn_sample_kernel(xn_ref, h_ref, wg_ref, wu_ref, cw_ref, cb_ref, wd_ref, p1_ref, p2_ref, o_ref, g_ref):
    _ffn_begin(h_ref, o_ref)
    t = lax.broadcasted_iota(jnp.int32, (xn_ref.shape[0], 1), 0) & (DEC_SEQ - 1)
    gate, up = _ffn_proj(xn_ref[...], wg_ref, wu_ref)
    g1 = jnp.where(t >= 1, pltpu.roll(gate, 1, 0), p1_ref[...])
    g2 = jnp.where(t >= 2, pltpu.roll(gate, 2, 0), p2_ref[...])
    g_ref[...] = gate
    o_ref[...] += _ffn_gated(gate, up, g2, g1, cw_ref[...], cb_ref[...], wd_ref)


def _ffn_common_specs(layer, tm, tf):
    nf = D_FF // tf
    return [
        pl.BlockSpec((tm, D_MODEL), lambda i, f: (i, 0), pipeline_mode=pl.Buffered(1)),
        pl.BlockSpec((tm, D_MODEL), lambda i, f: (i, 0), pipeline_mode=pl.Buffered(1)),
        pl.BlockSpec((None, D_MODEL, tf), lambda i, f: (layer, 0, f)),
        pl.BlockSpec((None, D_MODEL, tf), lambda i, f: (layer, 0, nf + f)),
        pl.BlockSpec((None, 3, tf), lambda i, f: (layer, 0, f)),
        pl.BlockSpec((None, 1, tf), lambda i, f: (layer, 0, f)),
        pl.BlockSpec((None, tf, D_MODEL), lambda i, f: (layer, f, 0)),
    ]


def ffn_prompt(xn, h, w_gu, conv_w, conv_b, w_down, layer, tm, tf):
    m = xn.shape[0]
    n_tiles = m // tm
    tiles_per_seq = LP // tm
    state_row = ((L_REAL - 2) % tm) // 8 * 8
    kern = functools.partial(_ffn_prompt_kernel, tiles_per_seq=tiles_per_seq, state_row=state_row,
                             n_sub=FFN_ROW_GROUPS)
    return pl.pallas_call(
        kern,
        grid=(n_tiles, D_FF // tf),
        in_specs=_ffn_common_specs(layer, tm, tf),
        out_specs=[pl.BlockSpec((tm, D_MODEL), lambda i, f: (i, 0)),
                   pl.BlockSpec((1, 8, tf), lambda i, f: (i, 0, f))],
        out_shape=[jax.ShapeDtypeStruct((m, D_MODEL), F32),
                   jax.ShapeDtypeStruct((n_tiles, 8, D_FF), F32)],
        scratch_shapes=[pltpu.VMEM((D_FF // tf, 8, tf), F32)],
        compiler_params=_params(("arbitrary", "arbitrary")),
        name="ffn_prompt",
    )(xn, h, w_gu, w_gu, conv_w, conv_b.reshape(DEPTH, 1, D_FF), w_down)


def ffn_sample(xn, h, w_gu, conv_w, conv_b, w_down, layer, p1, p2, tf):
    m = xn.shape[0]
    specs = _ffn_common_specs(layer, m, tf)
    specs += [pl.BlockSpec((m, tf), lambda i, f: (0, f)), pl.BlockSpec((m, tf), lambda i, f: (0, f))]
    return pl.pallas_call(
        _ffn_sample_kernel,
        grid=(1, D_FF // tf),
        in_specs=specs,
        out_specs=[pl.BlockSpec((m, D_MODEL), lambda i, f: (0, 0)),
                   pl.BlockSpec((m, tf), lambda i, f: (0, f))],
        out_shape=[jax.ShapeDtypeStruct((m, D_MODEL), F32),
                   jax.ShapeDtypeStruct((m, D_FF), F32)],
        compiler_params=_params(("arbitrary", "arbitrary")),
        name="ffn_sample",
    )(xn, h, w_gu, w_gu, conv_w, conv_b.reshape(DEPTH, 1, D_FF), w_down, p1, p2)


TM_PROMPT = LP // 2
TM_NORM = LP // 4
TN = 512
TF = 512
FFN_ROW_GROUPS = 4


def _alibi(n):
    return jnp.asarray(np.power(2.0, -8.0 * np.arange(1, n + 1, dtype=np.float32) / n).astype(np.float32))


def kernel(x_prompt, x_sample, cache_a_k, cache_a_v, state_b_k, state_b_v, state_conv, page_table, meta,
           norm_mix, norm_ffn, norm_final, w_qkv_a, lambda_a, subln_a, w_o_a, w_qkv_b, b_qkv_b, sinks_b,
           w_o_b, b_o_b, w_gu, conv_w, conv_b, w_down):
    bp, bd = BATCH, DEC_BATCH
    slopes_a = _alibi(A_HEADS)
    slopes_b = _alibi(B_HEADS)
    w_qkv_a, w_o_a, w_qkv_b, w_o_b, w_gu, w_down = (
        w.astype(BF16) for w in (w_qkv_a, w_o_a, w_qkv_b, w_o_b, w_gu, w_down))
    n_pool = cache_a_k.shape[1]
    cache_k = cache_a_k.reshape(cache_a_k.shape[0], n_pool, PAGE * A_KV, 2 * A_DH)
    cache_v = cache_a_v.reshape(cache_a_v.shape[0], n_pool, PAGE * A_KV, 2 * A_DH)

    hp = jnp.concatenate([jnp.broadcast_to(meta[None], (bp, N_META, D_MODEL)), x_prompt,
                          jnp.zeros((bp, LP - L_REAL, D_MODEL), F32)], axis=1).reshape(M_PROMPT, D_MODEL)
    hs = x_sample.reshape(M_SAMPLE, D_MODEL)
    ka_p, va_p, ka_s, va_s = [], [], [], []
    kb_p, vb_p, kb_s, vb_s = [], [], [], []
    cv_p, cv_s = [], []
    for i in range(DEPTH):
        j = i // 2
        xnp = rmsnorm(hp, norm_mix[i], TM_NORM, BF16)
        xns = rmsnorm(hs, norm_mix[i], M_SAMPLE, BF16)
        if i % 2 == 0:
            lam_init = 0.8 - 0.6 * math.exp(-0.3 * i)
            n_qkv = w_qkv_a.shape[2]
            qkv_p = matmul(xnp, w_qkv_a, j, n_qkv, TM_PROMPT, TN).reshape(bp, LP, n_qkv)
            o_p = diff_prompt(qkv_p, lambda_a[j], subln_a[j], slopes_a, lam_init)
            hp = matmul(o_p.reshape(M_PROMPT, D_MODEL), w_o_a, j, D_MODEL, TM_PROMPT, TN, res=hp)
            ka_p.append(qkv_p[:, :L_REAL, 2048:3072].reshape(bp, L_REAL, A_KV, 2 * A_DH))
            va_p.append(qkv_p[:, :L_REAL, 3072:4096].reshape(bp, L_REAL, A_KV, 2 * A_DH))

            qkv_s = matmul(xns, w_qkv_a, j, n_qkv, M_SAMPLE, TN).reshape(bd, DEC_SEQ, n_qkv)
            q_r = qkv_s[:, :, :2048].reshape(bd, DEC_SEQ, A_KV, 2, 2 * A_DH)
            q_r = q_r.transpose(0, 2, 3, 1, 4).reshape(bd, A_KV, 2 * DEC_SEQ, 2 * A_DH)
            k_new = qkv_s[:, :, 2048:3072]
            v_new = qkv_s[:, :, 3072:4096]
            o_s = diff_decode(q_r, k_new, v_new, cache_k, cache_v, page_table, j,
                              lambda_a[j], subln_a[j], slopes_a, lam_init)
            o_s = o_s.reshape(bd, A_KV, 2, DEC_SEQ, 2 * A_DH).transpose(0, 3, 1, 2, 4).reshape(M_SAMPLE, D_MODEL)
            hs = matmul(o_s, w_o_a, j, D_MODEL, M_SAMPLE, TN, res=hs)
            ka_s.append(k_new.reshape(bd, DEC_SEQ, A_KV, 2 * A_DH))
            va_s.append(v_new.reshape(bd, DEC_SEQ, A_KV, 2 * A_DH))
        else:
            n_qkv = w_qkv_b.shape[2]
            kvw = B_KV * B_DH
            qkv_p = matmul(xnp, w_qkv_b, j, n_qkv, TM_PROMPT, TN, bias=b_qkv_b[j]).reshape(bp, LP, n_qkv)
            o_p = swa_prompt(qkv_p, sinks_b[j], slopes_b)
            hp = matmul(o_p.reshape(M_PROMPT, D_MODEL), w_o_b, j, D_MODEL, TM_PROMPT, TN, bias=b_o_b[j], res=hp)
            kb_p.append(qkv_p[:, L_REAL - WINDOW:L_REAL, 2048:2048 + kvw].reshape(bp, WINDOW, B_KV, B_DH))
            vb_p.append(qkv_p[:, L_REAL - WINDOW:L_REAL, 2048 + kvw:].reshape(bp, WINDOW, B_KV, B_DH))

            qkv_s = matmul(xns, w_qkv_b, j, n_qkv, M_SAMPLE, TN, bias=b_qkv_b[j]).reshape(bd, DEC_SEQ, n_qkv)
            q_r = qkv_s[:, :, :2048].reshape(bd, DEC_SEQ, B_KV, B_GROUP, B_DH)
            q_r = q_r.transpose(0, 2, 3, 1, 4).reshape(bd, B_KV, B_GROUP * DEC_SEQ, B_DH)
            o_s, k_st, v_st = swa_decode(q_r, state_b_k[j].reshape(bd, WINDOW, kvw),
                                         state_b_v[j].reshape(bd, WINDOW, kvw),
                                         qkv_s[:, :, 2048:2048 + kvw], qkv_s[:, :, 2048 + kvw:],
                                         sinks_b[j], slopes_b)
            o_s = o_s.reshape(bd, B_KV, B_GROUP, DEC_SEQ, B_DH).transpose(0, 3, 1, 2, 4).reshape(M_SAMPLE, D_MODEL)
            hs = matmul(o_s, w_o_b, j, D_MODEL, M_SAMPLE, TN, bias=b_o_b[j], res=hs)
            kb_s.append(k_st.reshape(bd, WINDOW, B_KV, B_DH))
            vb_s.append(v_st.reshape(bd, WINDOW, B_KV, B_DH))

        xnp = rmsnorm(hp, norm_ffn[i], TM_NORM, BF16)
        hp, tails = ffn_prompt(xnp, hp, w_gu, conv_w, conv_b, w_down, i, TM_PROMPT, TF)
        off = (L_REAL - 2) % TM_PROMPT % 8
        cv_p.append(tails[LP // TM_PROMPT - 1::LP // TM_PROMPT, off:off + 2])

        xns = rmsnorm(hs, norm_ffn[i], M_SAMPLE, BF16)
        st = state_conv[i]
        zero = jnp.zeros((bd, 1, D_FF), F32)
        p1 = jnp.concatenate([st[:, 1:2], zero, zero, zero], axis=1).reshape(M_SAMPLE, D_FF)
        p2 = jnp.concatenate([st[:, 0:1], st[:, 1:2], zero, zero], axis=1).reshape(M_SAMPLE, D_FF)
        hs, gate_s = ffn_sample(xns, hs, w_gu, conv_w, conv_b, w_down, i, p1, p2, TF)
        cv_s.append(gate_s.reshape(bd, DEC_SEQ, D_FF)[:, DEC_SEQ - 2:])

    y_prompt = rmsnorm(hp, norm_final, TM_NORM, F32).reshape(bp, LP, D_MODEL)[:, N_META:L_REAL]
    y_sample = rmsnorm(hs, norm_final, M_SAMPLE, F32).reshape(bd, DEC_SEQ, D_MODEL)
    return (y_prompt, y_sample, jnp.stack(ka_p), jnp.stack(va_p), jnp.stack(ka_s), jnp.stack(va_s),
            jnp.stack(kb_p), jnp.stack(vb_p), jnp.stack(kb_s), jnp.stack(vb_s),
            jnp.stack(cv_p), jnp.stack(cv_s))
```

```python
import functools
import math

import numpy as np
import jax
import jax.numpy as jnp
from jax import lax
from jax.experimental import pallas as pl
from jax.experimental.pallas import tpu as pltpu

F32 = jnp.float32
BF16 = jnp.bfloat16

D_MODEL = 2048
BATCH = 4
SEQ = 2048
DEPTH = 4
DEC_BATCH = 32
DEC_SEQ = 4
PAGE = 128
N_META = 16
EPS = 1e-6
NEG_INF = -1e30
LOG2E = math.log2(math.e)
A_HEADS = 16
A_KV = 8
A_DH = 64
B_HEADS = 32
B_KV = 8
B_GROUP = 4
B_DH = 64
WINDOW = 128
D_FF = 5632

L_REAL = N_META + SEQ
BLK = 128
LP = -(-L_REAL // BLK) * BLK
N_BLK = LP // BLK
M_PROMPT = BATCH * LP
M_SAMPLE = DEC_BATCH * DEC_SEQ

VMEM_LIMIT = 56 * 1024 * 1024


def _params(sem, vmem=VMEM_LIMIT):
    return pltpu.CompilerParams(dimension_semantics=sem, vmem_limit_bytes=vmem)


def _rmsnorm_kernel(x_ref, g_ref, o_ref):
    x = x_ref[...]
    ms = jnp.mean(x * x, axis=-1, keepdims=True)
    o_ref[...] = (x * lax.rsqrt(ms + EPS) * g_ref[...]).astype(o_ref.dtype)


def rmsnorm(x, g, tm, out_dtype):
    m, d = x.shape
    return pl.pallas_call(
        _rmsnorm_kernel,
        grid=(m // tm,),
        in_specs=[pl.BlockSpec((tm, d), lambda i: (i, 0)),
                  pl.BlockSpec((1, d), lambda i: (0, 0))],
        out_specs=pl.BlockSpec((tm, d), lambda i: (i, 0)),
        out_shape=jax.ShapeDtypeStruct((m, d), out_dtype),
        compiler_params=_params(("parallel",)),
        name="rmsnorm",
    )(x, g.reshape(1, d))


def _matmul_kernel(*refs, has_bias, has_res):
    x_ref, w_ref = refs[0], refs[1]
    pos = 2
    b_ref = r_ref = None
    if has_bias:
        b_ref = refs[pos]
        pos += 1
    if has_res:
        r_ref = refs[pos]
        pos += 1
    o_ref = refs[pos]
    acc = jnp.dot(x_ref[...].astype(BF16), w_ref[...], preferred_element_type=F32)
    if has_bias:
        acc = acc + b_ref[...]
    if has_res:
        acc = acc + r_ref[...]
    o_ref[...] = acc


def matmul(x, w, layer, n_out, tm, tn, bias=None, res=None):
    m, k = x.shape
    in_specs = [pl.BlockSpec((tm, k), lambda j, i: (i, 0)),
                pl.BlockSpec((None, k, tn), lambda j, i: (layer, 0, j))]
    args = [x, w]
    if bias is not None:
        in_specs.append(pl.BlockSpec((1, tn), lambda j, i: (0, j)))
        args.append(bias.reshape(1, n_out))
    if res is not None:
        in_specs.append(pl.BlockSpec((tm, tn), lambda j, i: (i, j)))
        args.append(res)
    return pl.pallas_call(
        functools.partial(_matmul_kernel, has_bias=bias is not None, has_res=res is not None),
        grid=(n_out // tn, m // tm),
        in_specs=in_specs,
        out_specs=pl.BlockSpec((tm, tn), lambda j, i: (i, j)),
        out_shape=jax.ShapeDtypeStruct((m, n_out), F32),
        compiler_params=_params(("arbitrary", "arbitrary")),
        name="matmul",
    )(*args)


def _norm_matmul_kernel(*refs, has_bias):
    g_ref, h_ref, w_ref = refs[:3]
    b_ref = refs[3] if has_bias else None
    o_ref, xn_ref = refs[-2], refs[-1]

    @pl.when(pl.program_id(1) == 0)
    def _():
        x = h_ref[...]
        ms = jnp.mean(x * x, axis=-1, keepdims=True)
        xn_ref[...] = (x * lax.rsqrt(ms + EPS) * g_ref[...]).astype(xn_ref.dtype)

    acc = jnp.dot(xn_ref[...], w_ref[...], preferred_element_type=F32)
    if has_bias:
        acc = acc + b_ref[...]
    o_ref[...] = acc


def norm_matmul(h, g, w, layer, n_out, tm, tn, bias=None):
    m, k = h.shape
    in_specs = [pl.BlockSpec((1, k), lambda i, j: (0, 0)),
                pl.BlockSpec((tm, k), lambda i, j: (i, 0)),
                pl.BlockSpec((None, k, tn), lambda i, j: (layer, 0, j))]
    args = [g.reshape(1, k), h, w]
    if bias is not None:
        in_specs.append(pl.BlockSpec((1, tn), lambda i, j: (0, j)))
        args.append(bias.reshape(1, n_out))
    return pl.pallas_call(
        functools.partial(_norm_matmul_kernel, has_bias=bias is not None),
        grid=(m // tm, n_out // tn),
        in_specs=in_specs,
        out_specs=pl.BlockSpec((tm, tn), lambda i, j: (i, j)),
        out_shape=jax.ShapeDtypeStruct((m, n_out), F32),
        scratch_shapes=[pltpu.VMEM((tm, k), BF16)],
        compiler_params=_params(("arbitrary", "arbitrary")),
        name="norm_matmul",
    )(*args)


def _diff_lambda(lam_ref, lam_init):
    lp = lam_ref[...]
    a = jnp.sum(lp[0:1] * lp[1:2], axis=-1, keepdims=True)
    b = jnp.sum(lp[2:3] * lp[3:4], axis=-1, keepdims=True)
    return jnp.exp(a) - jnp.exp(b) + lam_init


def _subln(o, sub_ref, lam_init):
    ms = jnp.mean(o * o, axis=-1, keepdims=True)
    return (o * lax.rsqrt(ms + EPS) * sub_ref[...]) * (1.0 - lam_init)


A_HPS = 2
A_CHUNKS = 2 * A_HPS


def _diff_prompt_kernel(slopes_ref, q_ref, k_ref, v_ref, lam_ref, sub_ref, o_ref,
                        kb_ref, vt_ref, qt_ref, m_ref, l_ref, acc_ref, *, lam_init):
    hp = pl.program_id(1)
    i = pl.program_id(2)

    @pl.when(i == 0)
    def _():
        for hh in range(A_HPS):
            for jj in range(N_BLK):
                kb_ref[hh * N_BLK + jj] = k_ref[0, jj * BLK:(jj + 1) * BLK, hh * BLK:(hh + 1) * BLK].astype(BF16)
                vt_ref[hh * N_BLK + jj] = v_ref[0, jj * BLK:(jj + 1) * BLK, hh * BLK:(hh + 1) * BLK].T.astype(BF16)

    d_io = lax.broadcasted_iota(jnp.int32, (BLK, BLK), 0)
    for ch in range(A_CHUNKS):
        qgt = (q_ref[0, :, ch * BLK:(ch + 1) * BLK] * (A_DH ** -0.5 * LOG2E)).T
        qt_ref[:, (2 * ch) * BLK:(2 * ch + 1) * BLK] = jnp.where(d_io < A_DH, qgt, 0.0).astype(BF16)
        qt_ref[:, (2 * ch + 1) * BLK:(2 * ch + 2) * BLK] = jnp.where(d_io >= A_DH, qgt, 0.0).astype(BF16)
    m_ref[...] = jnp.full(m_ref.shape, NEG_INF, F32)
    l_ref[...] = jnp.zeros(l_ref.shape, F32)
    acc_ref[...] = jnp.zeros(acc_ref.shape, F32)

    k_io = lax.broadcasted_iota(jnp.int32, (BLK, 2 * BLK), 0)
    c_io = lax.broadcasted_iota(jnp.int32, (BLK, 2 * BLK), 1)
    rel = ((c_io & (BLK - 1)) - k_io).astype(F32)
    slope = [slopes_ref[A_CHUNKS * hp + ch] * LOG2E for ch in range(A_CHUNKS)]
    srel = [slope[ch] * rel for ch in range(A_CHUNKS)]
    cols = [slice(2 * ch * BLK, (2 * ch + 2) * BLK) for ch in range(A_CHUNKS)]

    def scores(j, masked):
        ts = []
        for ch in range(A_CHUNKS):
            kj = kb_ref[(ch // 2) * N_BLK + j]
            t = jnp.dot(kj, qt_ref[:, cols[ch]], preferred_element_type=F32) - srel[ch]
            ts.append(jnp.where(rel >= 0, t, NEG_INF) if masked else t)
        return ts

    def softmax(j, ts):
        off = ((i - j) * BLK).astype(F32)
        ps = []
        alphas = []
        for ch in range(A_CHUNKS):
            shift = slope[ch] * off
            m_old = m_ref[:, cols[ch]]
            m_new = jnp.maximum(m_old, jnp.max(ts[ch], axis=0, keepdims=True) - shift)
            alpha = jnp.exp2(m_old - m_new)
            p = jnp.exp2(ts[ch] - (m_new + shift))
            l_ref[:, cols[ch]] = alpha * l_ref[:, cols[ch]] + jnp.sum(p, axis=0, keepdims=True)
            m_ref[:, cols[ch]] = m_new
            ps.append(p.astype(BF16))
            alphas.append(alpha)
        return ps, alphas

    def values(j, ps, alphas):
        for ch in range(A_CHUNKS):
            vtj = vt_ref[(ch // 2) * N_BLK + j]
            acc_ref[:, cols[ch]] = (alphas[ch] * acc_ref[:, cols[ch]]
                                    + jnp.dot(vtj, ps[ch], preferred_element_type=F32))

    def group(start, count, mask_last):
        ts = scores(start, mask_last and count == 1)
        for u in range(count):
            nxt = scores(start + u + 1, mask_last and u + 2 == count) if u + 1 < count else None
            ps, alphas = softmax(start + u, ts)
            values(start + u, ps, alphas)
            ts = nxt

    n = i + 1
    nq = n // 4
    r = n - 4 * nq
    for rr in (1, 2, 3):
        @pl.when(jnp.logical_and(r == rr, nq == 0))
        def _(rr=rr):
            group(0, rr, True)

        @pl.when(jnp.logical_and(r == rr, nq > 0))
        def _(rr=rr):
            group(0, rr, False)

    def body(q, carry):
        group(r + 4 * q, 4, False)
        return carry

    lax.fori_loop(0, nq - 1, body, 0)

    @pl.when(nq > 0)
    def _():
        group(r + 4 * (nq - 1), 4, True)

    lam = _diff_lambda(lam_ref, lam_init)
    ot = acc_ref[...] / l_ref[...]
    for ch in range(A_CHUNKS):
        ogt = ot[:, (2 * ch) * BLK:(2 * ch + 1) * BLK] - lam * ot[:, (2 * ch + 1) * BLK:(2 * ch + 2) * BLK]
        o_ref[0, :, ch * BLK:(ch + 1) * BLK] = _subln(ogt.T, sub_ref, lam_init).astype(o_ref.dtype)


def diff_prompt(qkv, lam_p, subln, slopes, lam_init):
    b = qkv.shape[0]
    kern = functools.partial(_diff_prompt_kernel, lam_init=lam_init)
    qw = A_CHUNKS * BLK
    kw = A_HPS * BLK
    k_blk = 2 * A_KV * BLK // kw
    v_blk = k_blk + A_KV * BLK // kw
    return pl.pallas_call(
        kern,
        grid=(b, A_KV // A_HPS, N_BLK),
        in_specs=[
            pl.BlockSpec(memory_space=pltpu.SMEM),
            pl.BlockSpec((1, BLK, qw), lambda bb, h, i: (bb, i, h)),
            pl.BlockSpec((1, LP, kw), lambda bb, h, i: (bb, 0, k_blk + h)),
            pl.BlockSpec((1, LP, kw), lambda bb, h, i: (bb, 0, v_blk + h)),
            pl.BlockSpec((4, A_DH), lambda bb, h, i: (0, 0)),
            pl.BlockSpec((1, BLK), lambda bb, h, i: (0, 0)),
        ],
        out_specs=pl.BlockSpec((1, BLK, qw), lambda bb, h, i: (bb, i, h)),
        out_shape=jax.ShapeDtypeStruct((b, LP, 2 * A_KV * BLK), BF16),
        scratch_shapes=[pltpu.VMEM((A_HPS * N_BLK, BLK, BLK), BF16),
                        pltpu.VMEM((A_HPS * N_BLK, BLK, BLK), BF16),
                        pltpu.VMEM((BLK, 2 * qw), BF16),
                        pltpu.VMEM((1, 2 * qw), F32),
                        pltpu.VMEM((1, 2 * qw), F32),
                        pltpu.VMEM((BLK, 2 * qw), F32)],
        compiler_params=_params(("arbitrary", "arbitrary", "arbitrary")),
        name="diff_prompt",
    )(slopes, qkv, qkv, qkv, lam_p, subln.reshape(1, BLK))


PAGES_PER_STEP = 16
DEC_ROWS = A_KV * 16


def _diff_decode_kernel(*refs, lam_init, past):
    pt_ref, q_ref, kn_ref, vn_ref, slope_ref, tq_ref = refs[:6]
    kp = refs[6:6 + PAGES_PER_STEP]
    vp = refs[6 + PAGES_PER_STEP:6 + 2 * PAGES_PER_STEP]
    lam_ref, sub_ref, o_ref, qs_ref, m_ref, l_ref, acc_ref, knp_ref, vnp_ref = refs[6 + 2 * PAGES_PER_STEP:]
    del pt_ref
    c = pl.program_id(1)
    n_c = pl.num_programs(1)
    chunk = PAGES_PER_STEP * PAGE
    nt = (((1,), (1,)), ((), ()))

    @pl.when(c == 0)
    def _():
        lane8 = lax.broadcasted_iota(jnp.int32, (8, BLK), 1)
        for h in range(A_KV):
            q8 = q_ref[0, h] * (A_DH ** -0.5)
            qs_ref[h, 0:8, :] = jnp.where(lane8 < A_DH, q8, 0.0).astype(BF16)
            qs_ref[h, 8:16, :] = jnp.where(lane8 >= A_DH, q8, 0.0).astype(BF16)
        m_ref[...] = jnp.full(m_ref.shape, NEG_INF, F32)
        l_ref[...] = jnp.zeros(l_ref.shape, F32)
        acc_ref[...] = jnp.zeros(acc_ref.shape, F32)

    def update(s_tiles, v_of):
        mx = s_tiles[0]
        for s in s_tiles[1:]:
            mx = jnp.maximum(mx, s)
        m_old = m_ref[...]
        m_new = jnp.maximum(m_old, jnp.max(mx, axis=-1, keepdims=True))
        alpha = jnp.exp(m_old - m_new)
        m_b = jnp.broadcast_to(m_new, (DEC_ROWS, BLK))
        ps = [jnp.exp(s - m_b) for s in s_tiles]
        psum = ps[0]
        for p in ps[1:]:
            psum = psum + p
        l_ref[...] = alpha * l_ref[...] + jnp.sum(psum, axis=-1, keepdims=True)
        pv_rows = []
        for h in range(A_KV):
            pv = None
            for n, p in enumerate(ps):
                part = jnp.dot(p[h * 16:(h + 1) * 16].astype(BF16), v_of(n, h), preferred_element_type=F32)
                pv = part if pv is None else pv + part
            pv_rows.append(pv)
        acc_ref[...] = alpha * acc_ref[...] + jnp.concatenate(pv_rows, axis=0)
        m_ref[...] = m_new

    slope = slope_ref[...]
    tq = tq_ref[...]
    kcol = lax.broadcasted_iota(jnp.int32, (1, PAGE), 1)
    s_tiles = []
    for n in range(PAGES_PER_STEP):
        rows = []
        for h in range(A_KV):
            kh = kp[n][pl.ds(h, PAGE, stride=A_KV), :].astype(BF16)
            rows.append(lax.dot_general(qs_ref[h], kh, nt, preferred_element_type=F32))
        k_pos = (c * chunk + n * PAGE + kcol).astype(F32)
        s_tiles.append(jnp.concatenate(rows, axis=0) - slope * (tq - k_pos))
    update(s_tiles, lambda n, h: vp[n][pl.ds(h, PAGE, stride=A_KV), :].astype(BF16))

    @pl.when(c == n_c - 1)
    def _():
        knp_ref[...] = jnp.zeros(knp_ref.shape, F32)
        vnp_ref[...] = jnp.zeros(vnp_ref.shape, F32)
        knp_ref[0:DEC_SEQ, :] = kn_ref[0]
        vnp_ref[0:DEC_SEQ, :] = vn_ref[0]
        rows = []
        for h in range(A_KV):
            kh = knp_ref[:, h * BLK:(h + 1) * BLK].astype(BF16)
            rows.append(lax.dot_general(qs_ref[h], kh, nt, preferred_element_type=F32))
        distn = tq - (float(past) + kcol.astype(F32))
        s_new = jnp.where(distn >= 0, jnp.concatenate(rows, axis=0) - slope * distn, NEG_INF)
        update([s_new], lambda n, h: vnp_ref[:, h * BLK:(h + 1) * BLK].astype(BF16))
        lam = _diff_lambda(lam_ref, lam_init)
        o = acc_ref[...] / l_ref[...]
        for h in range(A_KV):
            og = o[h * 16:h * 16 + 8] - lam * o[h * 16 + 8:h * 16 + 16]
            o_ref[0, h] = _subln(og, sub_ref, lam_init)


def diff_decode(q_r, k_new, v_new, cache_k, cache_v, page_table, layer, lam_p, subln, slopes, lam_init):
    bd = q_r.shape[0]
    n_pages = page_table.shape[1]
    past = n_pages * PAGE
    n_c = n_pages // PAGES_PER_STEP
    r = np.arange(DEC_ROWS)
    slope_rows = jnp.broadcast_to(slopes[2 * (r // 16) + (r // DEC_SEQ) % 2][:, None], (DEC_ROWS, BLK))
    tq_rows = jnp.asarray(np.broadcast_to((past + r % DEC_SEQ).astype(np.float32)[:, None], (DEC_ROWS, BLK)))

    def page_spec(n):
        return pl.BlockSpec((None, None, PAGE * A_KV, BLK),
                            lambda b, c, pt: (layer, pt[b, c * PAGES_PER_STEP + n], 0, 0))

    const_spec = pl.BlockSpec((DEC_ROWS, BLK), lambda b, c, pt: (0, 0))
    in_specs = [
        pl.BlockSpec((1, A_KV, 8, BLK), lambda b, c, pt: (b, 0, 0, 0)),
        pl.BlockSpec((1, DEC_SEQ, A_KV * BLK), lambda b, c, pt: (b, 0, 0)),
        pl.BlockSpec((1, DEC_SEQ, A_KV * BLK), lambda b, c, pt: (b, 0, 0)),
        const_spec, const_spec,
    ]
    in_specs += [page_spec(n) for n in range(PAGES_PER_STEP)]
    in_specs += [page_spec(n) for n in range(PAGES_PER_STEP)]
    in_specs += [pl.BlockSpec((4, A_DH), lambda b, c, pt: (0, 0)),
                 pl.BlockSpec((1, BLK), lambda b, c, pt: (0, 0))]
    kern = functools.partial(_diff_decode_kernel, lam_init=lam_init, past=past)
    return pl.pallas_call(
        kern,
        grid_spec=pltpu.PrefetchScalarGridSpec(
            num_scalar_prefetch=1,
            grid=(bd, n_c),
            in_specs=in_specs,
            out_specs=pl.BlockSpec((1, A_KV, 8, BLK), lambda b, c, pt: (b, 0, 0, 0)),
            scratch_shapes=[pltpu.VMEM((A_KV, 16, BLK), BF16),
                            pltpu.VMEM((DEC_ROWS, 1), F32),
                            pltpu.VMEM((DEC_ROWS, 1), F32),
                            pltpu.VMEM((DEC_ROWS, BLK), F32),
                            pltpu.VMEM((PAGE, A_KV * BLK), F32),
                            pltpu.VMEM((PAGE, A_KV * BLK), F32)],
        ),
        out_shape=jax.ShapeDtypeStruct((bd, A_KV, 8, BLK), F32),
        compiler_params=_params(("arbitrary", "arbitrary")),
        name="diff_decode",
    )(page_table, q_r, k_new, v_new, slope_rows, tq_rows,
      *([cache_k] * PAGES_PER_STEP), *([cache_v] * PAGES_PER_STEP), lam_p, subln.reshape(1, BLK))


def _swa_head(qh, kh, vh, dist, valid, slope, sink):
    s = lax.dot_general(qh, kh, (((1,), (1,)), ((), ())), preferred_element_type=F32)
    s = jnp.where(valid, s - slope * dist, NEG_INF)
    m = jnp.maximum(jnp.max(s, axis=-1, keepdims=True), sink)
    p = jnp.exp(s - m)
    p = p / (jnp.sum(p, axis=-1, keepdims=True) + jnp.exp(sink - m))
    return jnp.dot(p.astype(BF16), vh, preferred_element_type=F32)


def _swa_prompt_kernel(slopes_ref, sinks_ref, q_ref, kp_ref, kc_ref, vp_ref, vc_ref, o_ref):
    i = pl.program_id(1)
    kk = jnp.concatenate([kp_ref[0], kc_ref[0]], axis=0).astype(BF16)
    vv = jnp.concatenate([vp_ref[0], vc_ref[0]], axis=0).astype(BF16)
    row = lax.broadcasted_iota(jnp.int32, (BLK, 2 * BLK), 0)
    col = lax.broadcasted_iota(jnp.int32, (BLK, 2 * BLK), 1)
    dist_i = row + BLK - col
    k_pos = (i - 1) * BLK + col
    valid = (dist_i >= 0) & (dist_i <= WINDOW) & (k_pos >= 0)
    dist = dist_i.astype(F32)
    nt = (((1,), (1,)), ((), ()))
    scores = []
    for kv in range(B_KV):
        qs = jnp.concatenate([q_ref[0, :, hh * B_DH:(hh + 1) * B_DH]
                              for hh in range(kv * B_GROUP, (kv + 1) * B_GROUP)], axis=0)
        qs = (qs * (B_DH ** -0.5)).astype(BF16)
        scores.append(lax.dot_general(qs, kk[:, kv * B_DH:(kv + 1) * B_DH], nt, preferred_element_type=F32))
    probs = []
    for kv in range(B_KV):
        ps = []
        for g in range(B_GROUP):
            hh = kv * B_GROUP + g
            s = scores[kv][g * BLK:(g + 1) * BLK]
            s = jnp.where(valid, s - slopes_ref[hh] * dist, NEG_INF)
            sink = sinks_ref[hh]
            m = jnp.maximum(jnp.max(s, axis=-1, keepdims=True), sink)
            p = jnp.exp(s - m)
            p = p / (jnp.sum(p, axis=-1, keepdims=True) + jnp.exp(sink - m))
            ps.append(p.astype(BF16))
        probs.append(jnp.concatenate(ps, axis=0))
    outs = [jnp.dot(probs[kv], vv[:, kv * B_DH:(kv + 1) * B_DH], preferred_element_type=F32)
            for kv in range(B_KV)]
    for pair in range(B_HEADS // 2):
        kv, g = (2 * pair) // B_GROUP, (2 * pair) % B_GROUP
        both = jnp.concatenate([outs[kv][g * BLK:(g + 1) * BLK], outs[kv][(g + 1) * BLK:(g + 2) * BLK]], axis=-1)
        o_ref[0, :, pair * BLK:(pair + 1) * BLK] = both.astype(o_ref.dtype)


def swa_prompt(qkv, sinks, slopes):
    b = qkv.shape[0]
    kvw = B_KV * B_DH

    def prev(bb, i):
        return jnp.maximum(i - 1, 0)

    return pl.pallas_call(
        _swa_prompt_kernel,
        grid=(b, N_BLK),
        in_specs=[
            pl.BlockSpec(memory_space=pltpu.SMEM),
            pl.BlockSpec(memory_space=pltpu.SMEM),
            pl.BlockSpec((1, BLK, D_MODEL), lambda bb, i: (bb, i, 0)),
            pl.BlockSpec((1, BLK, kvw), lambda bb, i: (bb, prev(bb, i), 4)),
            pl.BlockSpec((1, BLK, kvw), lambda bb, i: (bb, i, 4)),
            pl.BlockSpec((1, BLK, kvw), lambda bb, i: (bb, prev(bb, i), 5)),
            pl.BlockSpec((1, BLK, kvw), lambda bb, i: (bb, i, 5)),
        ],
        out_specs=pl.BlockSpec((1, BLK, D_MODEL), lambda bb, i: (bb, i, 0)),
        out_shape=jax.ShapeDtypeStruct((b, LP, D_MODEL), BF16),
        compiler_params=_params(("parallel", "parallel")),
        name="swa_prompt",
    )(slopes, sinks, qkv, qkv, qkv, qkv, qkv)


KPAD = 2 * WINDOW


def _swa_decode_kernel(slopes_ref, sinks_ref, q_ref, kb_ref, vb_ref, kn_ref, vn_ref,
                       o_ref, ko_ref, vo_ref, kf_ref, vf_ref):
    for buf, new, full, out in ((kb_ref, kn_ref, kf_ref, ko_ref), (vb_ref, vn_ref, vf_ref, vo_ref)):
        full[...] = jnp.zeros(full.shape, F32)
        full[0:WINDOW, :] = buf[0]
        full[WINDOW:WINDOW + DEC_SEQ, :] = new[0]
        out[0, 0:WINDOW - DEC_SEQ, :] = buf[0, DEC_SEQ:WINDOW, :]
        out[0, WINDOW - DEC_SEQ:WINDOW, :] = new[0]
    rows = B_GROUP * DEC_SEQ
    row = lax.broadcasted_iota(jnp.int32, (rows, KPAD), 0)
    col = lax.broadcasted_iota(jnp.int32, (rows, KPAD), 1)
    dist_i = (row & (DEC_SEQ - 1)) + WINDOW - col
    valid = (dist_i >= 0) & (dist_i <= WINDOW) & (col < WINDOW + DEC_SEQ)
    dist = dist_i.astype(F32)
    g_col = lax.broadcasted_iota(jnp.int32, (rows, 1), 0) >> 2
    for kv in range(B_KV):
        slope = jnp.zeros((rows, 1), F32)
        sink = jnp.zeros((rows, 1), F32)
        for g in range(B_GROUP):
            slope = jnp.where(g_col == g, slopes_ref[kv * B_GROUP + g], slope)
            sink = jnp.where(g_col == g, sinks_ref[kv * B_GROUP + g], sink)
        qh = (q_ref[0, kv] * (B_DH ** -0.5)).astype(BF16)
        kh = kf_ref[:, kv * B_DH:(kv + 1) * B_DH].astype(BF16)
        vh = vf_ref[:, kv * B_DH:(kv + 1) * B_DH].astype(BF16)
        o_ref[0, kv] = _swa_head(qh, kh, vh, dist, valid, slope, sink)


def swa_decode(q_r, k_buf, v_buf, k_new, v_new, sinks, slopes):
    bd = q_r.shape[0]
    kvw = B_KV * B_DH
    buf_spec = pl.BlockSpec((1, WINDOW, kvw), lambda b: (b, 0, 0))
    new_spec = pl.BlockSpec((1, DEC_SEQ, kvw), lambda b: (b, 0, 0))
    return pl.pallas_call(
        _swa_decode_kernel,
        grid=(bd,),
        in_specs=[
            pl.BlockSpec(memory_space=pltpu.SMEM),
            pl.BlockSpec(memory_space=pltpu.SMEM),
            pl.BlockSpec((1, B_KV, B_GROUP * DEC_SEQ, B_DH), lambda b: (b, 0, 0, 0)),
            buf_spec, buf_spec, new_spec, new_spec,
        ],
        out_specs=[pl.BlockSpec((1, B_KV, B_GROUP * DEC_SEQ, B_DH), lambda b: (b, 0, 0, 0)),
                   buf_spec, buf_spec],
        out_shape=[jax.ShapeDtypeStruct((bd, B_KV, B_GROUP * DEC_SEQ, B_DH), F32),
                   jax.ShapeDtypeStruct((bd, WINDOW, kvw), F32),
                   jax.ShapeDtypeStruct((bd, WINDOW, kvw), F32)],
        scratch_shapes=[pltpu.VMEM((KPAD, kvw), F32), pltpu.VMEM((KPAD, kvw), F32)],
        compiler_params=_params(("parallel",)),
        name="swa_decode",
    )(slopes, sinks, q_r, k_buf, v_buf, k_new, v_new)


def _silu(x):
    return x * (1.0 / (1.0 + jnp.exp(-x)))


def _ffn_begin(g_ref, h_ref, o_ref, xn_ref):
    @pl.when(pl.program_id(1) == 0)
    def _():
        x = h_ref[...]
        o_ref[...] = x
        ms = jnp.mean(x * x, axis=-1, keepdims=True)
        xn_ref[...] = (x * lax.rsqrt(ms + EPS) * g_ref[...]).astype(xn_ref.dtype)


def _ffn_proj(xn, wg_ref, wu_ref):
    return (jnp.dot(xn, wg_ref[...], preferred_element_type=F32),
            jnp.dot(xn, wu_ref[...], preferred_element_type=F32))


def _ffn_gated(gate, up, gate_prev2, gate_prev1, cw, cb, wd_ref):
    gc = cb + (cw[0:1] * gate_prev2 + cw[1:2] * gate_prev1 + cw[2:3] * gate)
    act = (_silu(gc) * up).astype(BF16)
    return jnp.dot(act, wd_ref[...], preferred_element_type=F32)


def _ffn_prompt_kernel(g_ref, h_ref, wg_ref, wu_ref, cw_ref, cb_ref, wd_ref, o_ref, st_ref,
                       tail_ref, xn_ref, *, tiles_per_seq, state_row, n_sub):
    i = pl.program_id(0)
    f = pl.program_id(1)
    _ffn_begin(g_ref, h_ref, o_ref, xn_ref)

    @pl.when(i % tiles_per_seq == 0)
    def _():
        tail_ref[f] = jnp.zeros(tail_ref.shape[1:], F32)

    cw = cw_ref[...]
    cb = cb_ref[...]
    rs = xn_ref.shape[0] // n_sub
    row = lax.broadcasted_iota(jnp.int32, (rs, 1), 0)
    prev = tail_ref[f]
    cur = _ffn_proj(xn_ref[0:rs, :], wg_ref, wu_ref)
    for r in range(n_sub):
        nxt = _ffn_proj(xn_ref[(r + 1) * rs:(r + 2) * rs, :], wg_ref, wu_ref) if r + 1 < n_sub else None
        gate, up = cur
        p1 = prev[7:8]
        p2 = prev[6:7]
        g1 = jnp.where(row >= 1, pltpu.roll(gate, 1, 0), p1)
        g2 = jnp.where(row >= 2, pltpu.roll(gate, 2, 0), jnp.where(row == 1, p1, p2))
        o_ref[r * rs:(r + 1) * rs, :] += _ffn_gated(gate, up, g2, g1, cw, cb, wd_ref)
        prev = gate[rs - 8:rs]
        if r * rs <= state_row < (r + 1) * rs:
            st_ref[0] = gate[state_row - r * rs:state_row - r * rs + 8]
        cur = nxt
    tail_ref[f] = prev


def _ffn_sample_kernel(g_ref, h_ref, wg_ref, wu_ref, cw_ref, cb_ref, wd_ref, p1_ref, p2_ref, o_ref, gate_ref,
                       xn_ref):
    _ffn_begin(g_ref, h_ref, o_ref, xn_ref)
    t = lax.broadcasted_iota(jnp.int32, (xn_ref.shape[0], 1), 0) & (DEC_SEQ - 1)
    gate, up = _ffn_proj(xn_ref[...], wg_ref, wu_ref)
    g1 = jnp.where(t >= 1, pltpu.roll(gate, 1, 0), p1_ref[...])
    g2 = jnp.where(t >= 2, pltpu.roll(gate, 2, 0), p2_ref[...])
    gate_ref[...] = gate
    o_ref[...] += _ffn_gated(gate, up, g2, g1, cw_ref[...], cb_ref[...], wd_ref)


def _ffn_common_specs(layer, tm, tf):
    nf = D_FF // tf
    return [
        pl.BlockSpec((1, D_MODEL), lambda i, f: (0, 0)),
        pl.BlockSpec((tm, D_MODEL), lambda i, f: (i, 0), pipeline_mode=pl.Buffered(1)),
        pl.BlockSpec((None, D_MODEL, tf), lambda i, f: (layer, 0, f)),
        pl.BlockSpec((None, D_MODEL, tf), lambda i, f: (layer, 0, nf + f)),
        pl.BlockSpec((None, 3, tf), lambda i, f: (layer, 0, f)),
        pl.BlockSpec((None, 1, tf), lambda i, f: (layer, 0, f)),
        pl.BlockSpec((None, tf, D_MODEL), lambda i, f: (layer, f, 0)),
    ]


def ffn_prompt(g, h, w_gu, conv_w, conv_b, w_down, layer, tm, tf):
    m = h.shape[0]
    n_tiles = m // tm
    tiles_per_seq = LP // tm
    state_row = ((L_REAL - 2) % tm) // 8 * 8
    kern = functools.partial(_ffn_prompt_kernel, tiles_per_seq=tiles_per_seq, state_row=state_row,
                             n_sub=FFN_ROW_GROUPS)
    return pl.pallas_call(
        kern,
        grid=(n_tiles, D_FF // tf),
        in_specs=_ffn_common_specs(layer, tm, tf),
        out_specs=[pl.BlockSpec((tm, D_MODEL), lambda i, f: (i, 0)),
                   pl.BlockSpec((1, 8, tf), lambda i, f: (i, 0, f))],
        out_shape=[jax.ShapeDtypeStruct((m, D_MODEL), F32),
                   jax.ShapeDtypeStruct((n_tiles, 8, D_FF), F32)],
        scratch_shapes=[pltpu.VMEM((D_FF // tf, 8, tf), F32),
                        pltpu.VMEM((tm, D_MODEL), BF16)],
        compiler_params=_params(("arbitrary", "arbitrary")),
        name="ffn_prompt",
    )(g.reshape(1, D_MODEL), h, w_gu, w_gu, conv_w, conv_b.reshape(DEPTH, 1, D_FF), w_down)


def ffn_sample(g, h, w_gu, conv_w, conv_b, w_down, layer, p1, p2, tf):
    m = h.shape[0]
    specs = _ffn_common_specs(layer, m, tf)
    specs += [pl.BlockSpec((m, tf), lambda i, f: (0, f)), pl.BlockSpec((m, tf), lambda i, f: (0, f))]
    return pl.pallas_call(
        _ffn_sample_kernel,
        grid=(1, D_FF // tf),
        in_specs=specs,
        out_specs=[pl.BlockSpec((m, D_MODEL), lambda i, f: (0, 0)),
                   pl.BlockSpec((m, tf), lambda i, f: (0, f))],
        out_shape=[jax.ShapeDtypeStruct((m, D_MODEL), F32),
                   jax.ShapeDtypeStruct((m, D_FF), F32)],
        scratch_shapes=[pltpu.VMEM((m, D_MODEL), BF16)],
        compiler_params=_params(("arbitrary", "arbitrary")),
        name="ffn_sample",
    )(g.reshape(1, D_MODEL), h, w_gu, w_gu, conv_w, conv_b.reshape(DEPTH, 1, D_FF), w_down, p1, p2)


TM_PROMPT = LP // 2
TM_NORM = LP // 4
TN = 512
TF = 512
FFN_ROW_GROUPS = 4


def _alibi(n):
    return jnp.asarray(np.power(2.0, -8.0 * np.arange(1, n + 1, dtype=np.float32) / n).astype(np.float32))


def kernel(x_prompt, x_sample, cache_a_k, cache_a_v, state_b_k, state_b_v, state_conv, page_table, meta,
           norm_mix, norm_ffn, norm_final, w_qkv_a, lambda_a, subln_a, w_o_a, w_qkv_b, b_qkv_b, sinks_b,
           w_o_b, b_o_b, w_gu, conv_w, conv_b, w_down):
    bp, bd = BATCH, DEC_BATCH
    slopes_a = _alibi(A_HEADS)
    slopes_b = _alibi(B_HEADS)
    w_qkv_a, w_o_a, w_qkv_b, w_o_b, w_gu, w_down = (
        w.astype(BF16) for w in (w_qkv_a, w_o_a, w_qkv_b, w_o_b, w_gu, w_down))
    n_pool = cache_a_k.shape[1]
    cache_k = cache_a_k.reshape(cache_a_k.shape[0], n_pool, PAGE * A_KV, 2 * A_DH)
    cache_v = cache_a_v.reshape(cache_a_v.shape[0], n_pool, PAGE * A_KV, 2 * A_DH)

    hp = jnp.concatenate([jnp.broadcast_to(meta[None], (bp, N_META, D_MODEL)), x_prompt,
                          jnp.zeros((bp, LP - L_REAL, D_MODEL), F32)], axis=1).reshape(M_PROMPT, D_MODEL)
    hs = x_sample.reshape(M_SAMPLE, D_MODEL)
    ka_p, va_p, ka_s, va_s = [], [], [], []
    kb_p, vb_p, kb_s, vb_s = [], [], [], []
    cv_p, cv_s = [], []
    for i in range(DEPTH):
        j = i // 2
        if i % 2 == 0:
            lam_init = 0.8 - 0.6 * math.exp(-0.3 * i)
            n_qkv = w_qkv_a.shape[2]
            qkv_p = norm_matmul(hp, norm_mix[i], w_qkv_a, j, n_qkv, TM_PROMPT, TN).reshape(bp, LP, n_qkv)
            o_p = diff_prompt(qkv_p, lambda_a[j], subln_a[j], slopes_a, lam_init)
            hp = matmul(o_p.reshape(M_PROMPT, D_MODEL), w_o_a, j, D_MODEL, TM_PROMPT, TN, res=hp)
            ka_p.append(qkv_p[:, :L_REAL, 2048:3072].reshape(bp, L_REAL, A_KV, 2 * A_DH))
            va_p.append(qkv_p[:, :L_REAL, 3072:4096].reshape(bp, L_REAL, A_KV, 2 * A_DH))

            qkv_s = norm_matmul(hs, norm_mix[i], w_qkv_a, j, n_qkv, M_SAMPLE, TN).reshape(bd, DEC_SEQ, n_qkv)
            q_r = qkv_s[:, :, :2048].reshape(bd, DEC_SEQ, A_KV, 2, 2 * A_DH)
            q_r = q_r.transpose(0, 2, 3, 1, 4).reshape(bd, A_KV, 2 * DEC_SEQ, 2 * A_DH)
            k_new = qkv_s[:, :, 2048:3072]
            v_new = qkv_s[:, :, 3072:4096]
            o_s = diff_decode(q_r, k_new, v_new, cache_k, cache_v, page_table, j,
                              lambda_a[j], subln_a[j], slopes_a, lam_init)
            o_s = o_s.reshape(bd, A_KV, 2, DEC_SEQ, 2 * A_DH).transpose(0, 3, 1, 2, 4).reshape(M_SAMPLE, D_MODEL)
            hs = matmul(o_s, w_o_a, j, D_MODEL, M_SAMPLE, TN, res=hs)
            ka_s.append(k_new.reshape(bd, DEC_SEQ, A_KV, 2 * A_DH))
            va_s.append(v_new.reshape(bd, DEC_SEQ, A_KV, 2 * A_DH))
        else:
            n_qkv = w_qkv_b.shape[2]
            kvw = B_KV * B_DH
            qkv_p = norm_matmul(hp, norm_mix[i], w_qkv_b, j, n_qkv, TM_PROMPT, TN,
                                bias=b_qkv_b[j]).reshape(bp, LP, n_qkv)
            o_p = swa_prompt(qkv_p, sinks_b[j], slopes_b)
            hp = matmul(o_p.reshape(M_PROMPT, D_MODEL), w_o_b, j, D_MODEL, TM_PROMPT, TN, bias=b_o_b[j], res=hp)
            kb_p.append(qkv_p[:, L_REAL - WINDOW:L_REAL, 2048:2048 + kvw].reshape(bp, WINDOW, B_KV, B_DH))
            vb_p.append(qkv_p[:, L_REAL - WINDOW:L_REAL, 2048 + kvw:].reshape(bp, WINDOW, B_KV, B_DH))

            qkv_s = norm_matmul(hs, norm_mix[i], w_qkv_b, j, n_qkv, M_SAMPLE, TN,
                                bias=b_qkv_b[j]).reshape(bd, DEC_SEQ, n_qkv)
            q_r = qkv_s[:, :, :2048].reshape(bd, DEC_SEQ, B_KV, B_GROUP, B_DH)
            q_r = q_r.transpose(0, 2, 3, 1, 4).reshape(bd, B_KV, B_GROUP * DEC_SEQ, B_DH)
            o_s, k_st, v_st = swa_decode(q_r, state_b_k[j].reshape(bd, WINDOW, kvw),
                                         state_b_v[j].reshape(bd, WINDOW, kvw),
                                         qkv_s[:, :, 2048:2048 + kvw], qkv_s[:, :, 2048 + kvw:],
                                         sinks_b[j], slopes_b)
            o_s = o_s.reshape(bd, B_KV, B_GROUP, DEC_SEQ, B_DH).transpose(0, 3, 1, 2, 4).reshape(M_SAMPLE, D_MODEL)
            hs = matmul(o_s, w_o_b, j, D_MODEL, M_SAMPLE, TN, bias=b_o_b[j], res=hs)
            kb_s.append(k_st.reshape(bd, WINDOW, B_KV, B_DH))
            vb_s.append(v_st.reshape(bd, WINDOW, B_KV, B_DH))

        hp, tails = ffn_prompt(norm_ffn[i], hp, w_gu, conv_w, conv_b, w_down, i, TM_PROMPT, TF)
        off = (L_REAL - 2) % TM_PROMPT % 8
        cv_p.append(tails[LP // TM_PROMPT - 1::LP // TM_PROMPT, off:off + 2])

        st = state_conv[i]
        zero = jnp.zeros((bd, 1, D_FF), F32)
        p1 = jnp.concatenate([st[:, 1:2], zero, zero, zero], axis=1).reshape(M_SAMPLE, D_FF)
        p2 = jnp.concatenate([st[:, 0:1], st[:, 1:2], zero, zero], axis=1).reshape(M_SAMPLE, D_FF)
        hs, gate_s = ffn_sample(norm_ffn[i], hs, w_gu, conv_w, conv_b, w_down, i, p1, p2, TF)
        cv_s.append(gate_s.reshape(bd, DEC_SEQ, D_FF)[:, DEC_SEQ - 2:])

    y_prompt = rmsnorm(hp, norm_final, TM_NORM, F32).reshape(bp, LP, D_MODEL)[:, N_META:L_REAL]
    y_sample = rmsnorm(hs, norm_final, M_SAMPLE, F32).reshape(bd, DEC_SEQ, D_MODEL)
    return (y_prompt, y_sample, jnp.stack(ka_p), jnp.stack(va_p), jnp.stack(ka_s), jnp.stack(va_s),
            jnp.stack(kb_p), jnp.stack(vb_p), jnp.stack(kb_s), jnp.stack(vb_s),
            jnp.stack(cv_p), jnp.stack(cv_s))
```
